```python
import jax, jax.numpy as jnp
from jax import lax
import numpy as np

D_MODEL = 4096
BATCH = 4
SEQ = 2048
DEPTH = 4
DEC_BATCH = 128
DEC_SEQ = 8
PAST_LEN = 16384
PAGE_SIZE = 128

N_MIXERS = 2
N_SSM_LAYERS = (DEPTH + N_MIXERS - 1) // N_MIXERS
N_GDN_LAYERS = DEPTH // N_MIXERS
SSM_GROUP = 16
SSM_GROUPS = D_MODEL // SSM_GROUP
SSM_STATE = 64
SSM_CHUNK = 128
GDN_HEAD_DIM = 128
GDN_QK_HEADS = D_MODEL // 256
GDN_V_HEADS = D_MODEL // 128
GDN_KEY_DIM = GDN_QK_HEADS * GDN_HEAD_DIM
GDN_VAL_DIM = GDN_V_HEADS * GDN_HEAD_DIM
GDN_CONV_DIM = 2 * GDN_KEY_DIM + GDN_VAL_DIM
GDN_IN_DIM = GDN_CONV_DIM + GDN_VAL_DIM + 2 * GDN_V_HEADS
GDN_CONV_W = 4
GDN_CHUNK = 64
MOE_GROUPS = 8
MOE_PER_GROUP = 8
MOE_EXPERTS = MOE_GROUPS * MOE_PER_GROUP
MOE_TOPK = 2
MOE_FF = D_MODEL // 8
MOE_BLOCK = 128
NORM_EPS = 1e-6

kernel_name = "hybrid_s5_gdn_hmoe_step"


def rms_norm(x, g):
    x32 = x.astype(jnp.float32)
    y = x32 * lax.rsqrt(jnp.mean(x32 * x32, -1, keepdims=True) + NORM_EPS)
    return (y * g.astype(jnp.float32)).astype(x.dtype)


def l2norm(x):
    return x * lax.rsqrt(jnp.sum(x * x, -1, keepdims=True) + 1e-6)


def s5_mixer(u, h0, lam_re, lam_im, log_dt, b_re, b_im, c_re, c_im, d_skip, w_glu, b_glu):
    f32 = jnp.float32
    bt, L, D = u.shape
    lam = lax.complex(lam_re.astype(f32), lam_im.astype(f32))
    dt = jnp.exp(log_dt.astype(f32))[:, None]
    lam_bar = jnp.exp(lam * dt)
    coef = (lam_bar - 1.0) / lam
    b_bar = coef[..., None] * lax.complex(b_re.astype(f32), b_im.astype(f32))
    bb_re, bb_im = jnp.real(b_bar), jnp.imag(b_bar)
    cr, ci = c_re.astype(f32), c_im.astype(f32)
    C = SSM_CHUNK if L % SSM_CHUNK == 0 else L
    N = L // C
    ug = jnp.swapaxes(u.astype(f32).reshape(bt, N, C, SSM_GROUPS, SSM_GROUP), 0, 1)
    h_init = lax.complex(h0[..., 0].astype(f32), h0[..., 1].astype(f32))

    def combine(e1, e2):
        a1, x1 = e1
        a2, x2 = e2
        return a1 * a2, a2 * x1 + x2

    def step(h, uc):
        bu = lax.complex(jnp.einsum('btgc,gpc->btgp', uc, bb_re),
                         jnp.einsum('btgc,gpc->btgp', uc, bb_im))
        bu = bu.at[:, 0].add(lam_bar * h)
        a = jnp.broadcast_to(lam_bar, bu.shape)
        _, hs = lax.associative_scan(combine, (a, bu), axis=1)
        y = (jnp.einsum('btgp,gcp->btgc', jnp.real(hs), cr)
             - jnp.einsum('btgp,gcp->btgc', jnp.imag(hs), ci))
        return hs[:, -1], y

    h_last, y = lax.scan(step, h_init, ug)
    y = jnp.swapaxes(y, 0, 1).reshape(bt, L, D) + d_skip.astype(f32) * u.astype(f32)
    z = jax.nn.gelu(y)
    out = z * jax.nn.sigmoid(z @ w_glu.astype(f32) + b_glu.astype(f32))
    new_h = jnp.stack([jnp.real(h_last), jnp.imag(h_last)], -1).astype(u.dtype)
    return out.astype(u.dtype), new_h


def gated_delta_chunked(q, k, v, g, beta, s0):
    bt, H, L, dk = q.shape
    dv = v.shape[-1]
    C = GDN_CHUNK if L % GDN_CHUNK == 0 else L
    N = L // C

    def chunks(t):
        return jnp.moveaxis(t.reshape((bt, H, N, C) + t.shape[3:]), 2, 0)

    incl = jnp.tril(jnp.ones((C, C), bool))
    strict = jnp.tril(jnp.ones((C, C), bool), -1)
    eye = jnp.eye(C, dtype=jnp.float32)

    def step(S, inp):
        qc, kc, vc, gc, bc = inp
        gcum = jnp.cumsum(gc, -1)
        gdiff = gcum[..., :, None] - gcum[..., None, :]
        decay = jnp.where(incl, jnp.exp(jnp.where(incl, gdiff, 0.0)), 0.0)
        kb = kc * bc[..., None]
        m = jnp.where(strict, jnp.einsum('bhid,bhjd->bhij', kb, kc) * decay, 0.0)
        T = lax.linalg.triangular_solve(eye + m, jnp.broadcast_to(eye, m.shape),
                                        left_side=True, lower=True, unit_diagonal=True)
        u = T @ (vc * bc[..., None])
        w = T @ (kb * jnp.exp(gcum)[..., None])
        v_new = u - w @ S
        attn = jnp.where(incl, jnp.einsum('bhid,bhjd->bhij', qc, kc) * decay, 0.0)
        o = (qc * jnp.exp(gcum)[..., None]) @ S + attn @ v_new
        g_last = gcum[..., -1]
        S_new = (S * jnp.exp(g_last)[..., None, None]
                 + jnp.einsum('bhcd,bhce->bhde', kc * jnp.exp(g_last[..., None] - gcum)[..., None], v_new))
        return S_new, o

    S, o = lax.scan(step, s0, (chunks(q), chunks(k), chunks(v), chunks(g), chunks(beta)))
    o = jnp.moveaxis(o, 0, 2).reshape(bt, H, L, dv)
    return o, S


def gdn_mixer(h, conv_buf, s0, w_in, conv_w, a_log, dt_bias, norm_g, w_out):
    f32 = jnp.float32
    bt, L, _ = h.shape
    proj = h @ w_in
    qkv = proj[..., :GDN_CONV_DIM]
    z = proj[..., GDN_CONV_DIM:GDN_CONV_DIM + GDN_VAL_DIM]
    b_raw = proj[..., GDN_CONV_DIM + GDN_VAL_DIM:GDN_CONV_DIM + GDN_VAL_DIM + GDN_V_HEADS]
    a_raw = proj[..., GDN_CONV_DIM + GDN_VAL_DIM + GDN_V_HEADS:]
    xp = jnp.concatenate([conv_buf.astype(qkv.dtype), qkv], axis=1)
    conv = xp[:, 0:L] * conv_w[0]
    for t in range(1, GDN_CONV_W):
        conv = conv + xp[:, t:t + L] * conv_w[t]
    conv = jax.nn.silu(conv.astype(f32))
    new_buf = xp[:, xp.shape[1] - (GDN_CONV_W - 1):]
    rep = GDN_V_HEADS // GDN_QK_HEADS
    q = conv[..., :GDN_KEY_DIM].reshape(bt, L, GDN_QK_HEADS, GDN_HEAD_DIM)
    k = conv[..., GDN_KEY_DIM:2 * GDN_KEY_DIM].reshape(bt, L, GDN_QK_HEADS, GDN_HEAD_DIM)
    v = conv[..., 2 * GDN_KEY_DIM:].reshape(bt, L, GDN_V_HEADS, GDN_HEAD_DIM)
    q = jnp.repeat(l2norm(q), rep, axis=2) * (GDN_HEAD_DIM ** -0.5)
    k = jnp.repeat(l2norm(k), rep, axis=2)
    beta = jax.nn.sigmoid(b_raw.astype(f32))
    g = -jnp.exp(a_log.astype(f32)) * jax.nn.softplus(a_raw.astype(f32) + dt_bias.astype(f32))
    tr = lambda t: jnp.swapaxes(t, 1, 2)
    o, s_new = gated_delta_chunked(tr(q), tr(k), tr(v), tr(g), tr(beta), s0.astype(f32))
    o = tr(o)
    o = o * lax.rsqrt(jnp.mean(o * o, -1, keepdims=True) + NORM_EPS) * norm_g.astype(f32)
    o = o * jax.nn.silu(z.astype(f32).reshape(bt, L, GDN_V_HEADS, GDN_HEAD_DIM))
    out = o.reshape(bt, L, GDN_VAL_DIM).astype(h.dtype) @ w_out
    return out, new_buf, s_new.astype(h.dtype)


def grouped_experts(xf, eidx, wts, w_gate, w_up, w_down):
    T, D = xf.shape
    A = T * MOE_TOPK
    flat_e = eidx.reshape(A)
    order = jnp.argsort(flat_e)
    e_sorted = flat_e[order]
    counts = jnp.zeros((MOE_EXPERTS,), jnp.int32).at[flat_e].add(1)
    padded = (counts + MOE_BLOCK - 1) // MOE_BLOCK * MOE_BLOCK
    pad_end = jnp.cumsum(padded)
    pad_start = pad_end - padded
    start = jnp.cumsum(counts) - counts
    dest = pad_start[e_sorted] + jnp.arange(A, dtype=jnp.int32) - start[e_sorted]
    NB = (A + MOE_EXPERTS * (MOE_BLOCK - 1)) // MOE_BLOCK
    R = NB * MOE_BLOCK
    row_tok = jnp.zeros((R,), jnp.int32).at[dest].set((order // MOE_TOPK).astype(jnp.int32))
    row_w = jnp.zeros((R,), jnp.float32).at[dest].set(wts.reshape(A)[order])
    blk_exp = jnp.minimum(jnp.searchsorted(pad_end, jnp.arange(NB, dtype=jnp.int32) * MOE_BLOCK, side='right'),
                          MOE_EXPERTS - 1)

    def body(acc, inp):
        e, tok, w = inp
        xb = xf[tok]
        hb = jax.nn.silu(xb @ w_gate[e]) * (xb @ w_up[e])
        yb = (hb @ w_down[e]).astype(jnp.float32) * w[:, None]
        return acc.at[tok].add(yb), None

    acc, _ = lax.scan(body, jnp.zeros((T, D), jnp.float32),
                      (blk_exp, row_tok.reshape(NB, MOE_BLOCK), row_w.reshape(NB, MOE_BLOCK)))
    return acc.astype(xf.dtype)


def hier_moe(h, w_grp, b_grp, w_exp, b_exp, w_gate, w_up, w_down):
    f32 = jnp.float32
    bt, L, D = h.shape
    xf = h.reshape(bt * L, D)
    T = xf.shape[0]
    p_grp = jax.nn.softmax((xf @ w_grp).astype(f32) + b_grp.astype(f32), axis=-1)
    pg, gsel = lax.top_k(p_grp, 1)
    e_logits = ((xf @ w_exp).astype(f32) + b_exp.astype(f32)).reshape(T, MOE_GROUPS, MOE_PER_GROUP)
    sel = jnp.take_along_axis(e_logits, gsel[:, :, None], axis=1)[:, 0]
    pe, esel = lax.top_k(jax.nn.softmax(sel, axis=-1), MOE_TOPK)
    wts = pg * pe / jnp.sum(pe, -1, keepdims=True)
    eidx = gsel * MOE_PER_GROUP + esel
    return grouped_experts(xf, eidx, wts, w_gate, w_up, w_down).reshape(bt, L, D)


def trunk(x, ssm0, conv0, gdn0, p):
    ssm_out, conv_out, gdn_out = [], [], []
    for i in range(DEPTH):
        j = i // N_MIXERS
        h = rms_norm(x, p['attn_norm'][i])
        if i % N_MIXERS == 0:
            mix, h_new = s5_mixer(h, ssm0[j], p['ssm_lambda_re'][j], p['ssm_lambda_im'][j], p['ssm_log_dt'][j],
                                  p['ssm_b_re'][j], p['ssm_b_im'][j], p['ssm_c_re'][j], p['ssm_c_im'][j],
                                  p['ssm_d'][j], p['ssm_w_glu'][j], p['ssm_b_glu'][j])
            ssm_out.append(h_new)
        else:
            mix, c_new, s_new = gdn_mixer(h, conv0[j], gdn0[j], p['gdn_w_in'][j], p['gdn_conv_w'][j],
                                          p['gdn_a_log'][j], p['gdn_dt_bias'][j], p['gdn_norm'][j],
                                          p['gdn_w_out'][j])
            conv_out.append(c_new)
            gdn_out.append(s_new)
        x = x + mix
        x = x + hier_moe(rms_norm(x, p['ffn_norm'][i]), p['moe_w_grp'][i], p['moe_b_grp'][i],
                         p['moe_w_exp'][i], p['moe_b_exp'][i], p['moe_w_gate'][i], p['moe_w_up'][i],
                         p['moe_w_down'][i])
    y = rms_norm(x, p['final_norm'])
    return y, jnp.stack(ssm_out), jnp.stack(conv_out), jnp.stack(gdn_out)


def setup_inputs(seed: int = 0) -> dict:
    key = jax.random.key(seed)
    ks = iter(jax.random.split(key, 40))
    f32 = jnp.float32
    nrm = lambda shape, scale: scale * jax.random.normal(next(ks), shape, f32)
    G, P = SSM_GROUPS, SSM_STATE
    n_idx = jnp.arange(P, dtype=f32)
    gdn_dt = jnp.exp(jax.random.uniform(next(ks), (N_GDN_LAYERS, GDN_V_HEADS), f32,
                                        float(np.log(1e-3)), float(np.log(1e-1))))
    return {
        'x_prompt': nrm((BATCH, SEQ, D_MODEL), 1.0),
        'x_sample': nrm((DEC_BATCH, DEC_SEQ, D_MODEL), 1.0),
        'state_ssm': nrm((N_SSM_LAYERS, DEC_BATCH, G, P, 2), 0.1),
        'state_conv': nrm((N_GDN_LAYERS, DEC_BATCH, GDN_CONV_W - 1, GDN_CONV_DIM), 1.0),
        'state_gdn': nrm((N_GDN_LAYERS, DEC_BATCH, GDN_V_HEADS, GDN_HEAD_DIM, GDN_HEAD_DIM), 0.05),
        'attn_norm': 1.0 + nrm((DEPTH, D_MODEL), 0.02),
        'ffn_norm': 1.0 + nrm((DEPTH, D_MODEL), 0.02),
        'final_norm': 1.0 + nrm((D_MODEL,), 0.02),
        'ssm_lambda_re': -0.5 + nrm((N_SSM_LAYERS, G, P), 0.01),
        'ssm_lambda_im': jnp.pi * n_idx + nrm((N_SSM_LAYERS, G, P), 0.01),
        'ssm_log_dt': jnp.log(jax.random.uniform(next(ks), (N_SSM_LAYERS, G), f32, 1e-3, 1e-1)),
        'ssm_b_re': nrm((N_SSM_LAYERS, G, P, SSM_GROUP), (2 * SSM_GROUP) ** -0.5),
        'ssm_b_im': nrm((N_SSM_LAYERS, G, P, SSM_GROUP), (2 * SSM_GROUP) ** -0.5),
        'ssm_c_re': nrm((N_SSM_LAYERS, G, SSM_GROUP, P), P ** -0.5),
        'ssm_c_im': nrm((N_SSM_LAYERS, G, SSM_GROUP, P), P ** -0.5),
        'ssm_d': nrm((N_SSM_LAYERS, D_MODEL), 1.0),
        'ssm_w_glu': nrm((N_SSM_LAYERS, D_MODEL, D_MODEL), D_MODEL ** -0.5),
        'ssm_b_glu': nrm((N_SSM_LAYERS, D_MODEL), 0.02),
        'gdn_w_in': nrm((N_GDN_LAYERS, D_MODEL, GDN_IN_DIM), D_MODEL ** -0.5),
        'gdn_conv_w': nrm((N_GDN_LAYERS, GDN_CONV_W, GDN_CONV_DIM), GDN_CONV_W ** -0.5),
        'gdn_a_log': jnp.log(jax.random.uniform(next(ks), (N_GDN_LAYERS, GDN_V_HEADS), f32, 1.0, 16.0)),
        'gdn_dt_bias': gdn_dt + jnp.log(-jnp.expm1(-gdn_dt)),
        'gdn_norm': 1.0 + nrm((N_GDN_LAYERS, GDN_HEAD_DIM), 0.02),
        'gdn_w_out': nrm((N_GDN_LAYERS, GDN_VAL_DIM, D_MODEL), GDN_VAL_DIM ** -0.5),
        'moe_w_grp': nrm((DEPTH, D_MODEL, MOE_GROUPS), D_MODEL ** -0.5),
        'moe_b_grp': nrm((DEPTH, MOE_GROUPS), 0.01),
        'moe_w_exp': nrm((DEPTH, D_MODEL, MOE_EXPERTS), D_MODEL ** -0.5),
        'moe_b_exp': nrm((DEPTH, MOE_EXPERTS), 0.01),
        'moe_w_gate': nrm((DEPTH, MOE_EXPERTS, D_MODEL, MOE_FF), D_MODEL ** -0.5),
        'moe_w_up': nrm((DEPTH, MOE_EXPERTS, D_MODEL, MOE_FF), D_MODEL ** -0.5),
        'moe_w_down': nrm((DEPTH, MOE_EXPERTS, MOE_FF, D_MODEL), MOE_FF ** -0.5),
    }


def reference(x_prompt, x_sample, state_ssm, state_conv, state_gdn, attn_norm, ffn_norm, final_norm,
              ssm_lambda_re, ssm_lambda_im, ssm_log_dt, ssm_b_re, ssm_b_im, ssm_c_re, ssm_c_im, ssm_d,
              ssm_w_glu, ssm_b_glu, gdn_w_in, gdn_conv_w, gdn_a_log, gdn_dt_bias, gdn_norm, gdn_w_out,
              moe_w_grp, moe_b_grp, moe_w_exp, moe_b_exp, moe_w_gate, moe_w_up, moe_w_down):
    p = dict(attn_norm=attn_norm, ffn_norm=ffn_norm, final_norm=final_norm,
             ssm_lambda_re=ssm_lambda_re, ssm_lambda_im=ssm_lambda_im, ssm_log_dt=ssm_log_dt,
             ssm_b_re=ssm_b_re, ssm_b_im=ssm_b_im, ssm_c_re=ssm_c_re, ssm_c_im=ssm_c_im, ssm_d=ssm_d,
             ssm_w_glu=ssm_w_glu, ssm_b_glu=ssm_b_glu, gdn_w_in=gdn_w_in, gdn_conv_w=gdn_conv_w,
             gdn_a_log=gdn_a_log, gdn_dt_bias=gdn_dt_bias, gdn_norm=gdn_norm, gdn_w_out=gdn_w_out,
             moe_w_grp=moe_w_grp, moe_b_grp=moe_b_grp, moe_w_exp=moe_w_exp, moe_b_exp=moe_b_exp,
             moe_w_gate=moe_w_gate, moe_w_up=moe_w_up, moe_w_down=moe_w_down)
    bp = x_prompt.shape[0]
    ssm0 = jnp.zeros((N_SSM_LAYERS, bp) + state_ssm.shape[2:], state_ssm.dtype)
    conv0 = jnp.zeros((N_GDN_LAYERS, bp) + state_conv.shape[2:], state_conv.dtype)
    gdn0 = jnp.zeros((N_GDN_LAYERS, bp) + state_gdn.shape[2:], state_gdn.dtype)
    y_prompt, ssm_p, conv_p, gdn_p = trunk(x_prompt, ssm0, conv0, gdn0, p)
    y_sample, ssm_s, conv_s, gdn_s = trunk(x_sample, state_ssm, state_conv, state_gdn, p)
    return (y_prompt, y_sample, ssm_p, conv_p, gdn_p, ssm_s, conv_s, gdn_s)
```

```python
import functools

import jax
import jax.numpy as jnp
from jax import lax
from jax.experimental import pallas as pl
from jax.experimental.pallas import tpu as pltpu

F32 = jnp.float32
BF16 = jnp.bfloat16
HI = lax.Precision.HIGHEST

LANES = 128
SUBLANES = 8
VMEM_LIMIT = 56 * 1024 * 1024
NORM_EPS = 1e-6
S5_STEPS = 8
MOE_ROWS = 512
MOE_TOPK = 2


def _cparams(sem):
    return pltpu.CompilerParams(dimension_semantics=sem, vmem_limit_bytes=VMEM_LIMIT)


def _pick(n, cands):
    for c in cands:
        if n % c == 0:
            return c
    return n


def _norm_kernel(*refs, has_delta, has_router, n_grp, n_per):
    it = iter(refs)
    x_ref = next(it)
    d_ref = next(it) if has_delta else None
    g_ref = next(it)
    if has_router:
        wr_ref, br_ref = next(it), next(it)
    xo_ref = next(it) if has_delta else None
    h_ref = next(it)
    if has_router:
        idx_ref, wts_ref = next(it), next(it)

    x = x_ref[...]
    if has_delta:
        x = x + d_ref[...]
        xo_ref[...] = x
    h = x * lax.rsqrt(jnp.mean(x * x, axis=-1, keepdims=True) + NORM_EPS) * g_ref[...]
    h_ref[...] = h.astype(h_ref.dtype)
    if not has_router:
        return

    logits = jnp.dot(h, wr_ref[...], precision=HI, preferred_element_type=F32) + br_ref[...]
    lane = lax.broadcasted_iota(jnp.int32, logits.shape, 1).astype(F32)
    neg = jnp.float32(-jnp.inf)
    big = jnp.float32(2 ** 20)
    is_grp = lane < n_grp
    lg = jnp.where(is_grp, logits, neg)
    gmax = jnp.max(lg, axis=-1, keepdims=True)
    gsel = jnp.min(jnp.where(lg == gmax, lane, big), axis=-1, keepdims=True)
    pg = 1.0 / jnp.sum(jnp.where(is_grp, jnp.exp(lg - gmax), 0.0), axis=-1, keepdims=True)
    lo = n_grp + gsel * n_per
    in_sel = (lane >= lo) & (lane < lo + n_per)
    le = jnp.where(in_sel, logits, neg)
    m1 = jnp.max(le, axis=-1, keepdims=True)
    i1 = jnp.min(jnp.where(le == m1, lane, big), axis=-1, keepdims=True)
    le2 = jnp.where(lane == i1, neg, le)
    m2 = jnp.max(le2, axis=-1, keepdims=True)
    i2 = jnp.min(jnp.where(le2 == m2, lane, big), axis=-1, keepdims=True)
    e2 = jnp.exp(m2 - m1)
    w1 = pg / (1.0 + e2)
    w2 = pg * e2 / (1.0 + e2)
    idx_ref[...] = jnp.where(lane == 0, i1 - n_grp, jnp.where(lane == 1, i2 - n_grp, 0.0)).astype(jnp.int32)
    wts_ref[...] = jnp.where(lane == 0, w1, jnp.where(lane == 1, w2, 0.0))


def _add_norm(x, delta, g, h_dtype, router=None):
    T, D = x.shape
    tm = _pick(T, (256, 128, 64, 32, 16, 8))
    row = pl.BlockSpec((tm, D), lambda i: (i, 0))
    full = lambda a: pl.BlockSpec(a.shape, lambda i: (0,) * a.ndim)
    ins, specs = [x], [row]
    if delta is not None:
        ins.append(delta)
        specs.append(row)
    g2 = g.reshape(1, D).astype(F32)
    ins.append(g2)
    specs.append(full(g2))
    outs, ospecs = [], []
    if delta is not None:
        outs.append(jax.ShapeDtypeStruct((T, D), F32))
        ospecs.append(row)
    outs.append(jax.ShapeDtypeStruct((T, D), h_dtype))
    ospecs.append(row)
    n_grp = n_per = 0
    if router is not None:
        wr, br, n_grp, n_per = router
        ins += [wr, br]
        specs += [full(wr), full(br)]
        small = pl.BlockSpec((tm, LANES), lambda i: (i, 0))
        outs += [jax.ShapeDtypeStruct((T, LANES), jnp.int32), jax.ShapeDtypeStruct((T, LANES), F32)]
        ospecs += [small, small]
    res = pl.pallas_call(
        functools.partial(_norm_kernel, has_delta=delta is not None, has_router=router is not None,
                          n_grp=n_grp, n_per=n_per),
        grid=(T // tm,), in_specs=specs, out_specs=ospecs, out_shape=outs,
        compiler_params=_cparams(("parallel",)))(*ins)
    res = list(res)
    x_new = res.pop(0) if delta is not None else x
    return (x_new, *res)


def _mm_kernel(*refs, mode):
    if mode == "plain":
        a_ref, w_ref, o_ref, wbf = refs
    elif mode == "res":
        a_ref, w_ref, r_ref, o_ref, wbf = refs
    else:
        a_ref, w_ref, r_ref, z_ref, b_ref, o_ref, wbf = refs

    @pl.when(pl.program_id(1) == 0)
    def _():
        wbf[...] = w_ref[...].astype(BF16)

    acc = jnp.dot(a_ref[...], wbf[...], preferred_element_type=F32)
    if mode == "plain":
        o_ref[...] = acc.astype(o_ref.dtype)
    elif mode == "res":
        o_ref[...] = r_ref[...] + acc
    else:
        o_ref[...] = r_ref[...] + z_ref[...] * jax.nn.sigmoid(acc + b_ref[...])


def _matmul(a, w, n_cols, mode="plain", col0=0, tn=512, res=None, z=None, bias=None, out_dtype=F32):
    M, K = a.shape
    tn = min(tn, n_cols)
    assert n_cols % tn == 0 and col0 % tn == 0
    tm = _pick(M, (1024, 512, 256, 128, 64, 32, 16))
    nb0 = col0 // tn
    a_spec = pl.BlockSpec((tm, K), lambda j, i: (i, 0))
    w_spec = pl.BlockSpec((K, tn), lambda j, i: (0, j + nb0))
    o_spec = pl.BlockSpec((tm, tn), lambda j, i: (i, j))
    ins, specs = [a, w], [a_spec, w_spec]
    if mode in ("res", "glu"):
        ins.append(res)
        specs.append(o_spec)
    if mode == "glu":
        ins += [z, bias.reshape(1, -1)]
        specs += [o_spec, pl.BlockSpec((1, tn), lambda j, i: (0, j))]
    return pl.pallas_call(
        functools.partial(_mm_kernel, mode=mode),
        grid=(n_cols // tn, M // tm), in_specs=specs, out_specs=o_spec,
        out_shape=jax.ShapeDtypeStruct((M, n_cols), out_dtype),
        scratch_shapes=[pltpu.VMEM((K, tn), BF16)],
        compiler_params=_cparams(("parallel", "arbitrary")))(*ins)


def _s5_params(lam_re, lam_im, log_dt, b_re, b_im, c_re, c_im, d_skip):
    G, P = lam_re.shape
    SG = b_re.shape[-1]
    n = S5_STEPS
    dt = jnp.exp(log_dt.astype(F32))[:, None]
    a, b = lam_re.astype(F32) * dt, lam_im.astype(F32) * dt
    lb_re, lb_im = jnp.exp(a) * jnp.cos(b), jnp.exp(a) * jnp.sin(b)
    nr, ni = lb_re - 1.0, lb_im
    den = lam_re * lam_re + lam_im * lam_im
    cf_re = (nr * lam_re + ni * lam_im) / den
    cf_im = (ni * lam_re - nr * lam_im) / den
    bb_re = cf_re[..., None] * b_re - cf_im[..., None] * b_im
    bb_im = cf_re[..., None] * b_im + cf_im[..., None] * b_re
    ks = jnp.arange(n + 1, dtype=F32)[:, None, None]
    pw_re = jnp.exp(ks * a) * jnp.cos(ks * b)
    pw_im = jnp.exp(ks * a) * jnp.sin(ks * b)
    pr, pi = pw_re[n - 1::-1][:n], pw_im[n - 1::-1][:n]
    t_re = pr[..., None] * bb_re[None] - pi[..., None] * bb_im[None]
    t_im = pr[..., None] * bb_im[None] + pi[..., None] * bb_re[None]
    B8 = jnp.concatenate([jnp.transpose(t_re, (1, 0, 3, 2)), jnp.transpose(t_im, (1, 0, 3, 2))], -1)
    B8 = B8.reshape(G, n * SG, 2 * P)
    cr, ci = c_re.astype(F32), c_im.astype(F32)
    qr, qi = pw_re[1:], pw_im[1:]
    m_re = cr[None] * qr[:, :, None, :] - ci[None] * qi[:, :, None, :]
    m_im = cr[None] * qi[:, :, None, :] + ci[None] * qr[:, :, None, :]
    top = jnp.concatenate([jnp.transpose(m_re, (1, 3, 0, 2)), -jnp.transpose(m_im, (1, 3, 0, 2))], 1)
    top = top.reshape(G, 2 * P, n * SG)
    w_re = pw_re[:n, :, :, None] * bb_re[None] - pw_im[:n, :, :, None] * bb_im[None]
    w_im = pw_re[:n, :, :, None] * bb_im[None] + pw_im[:n, :, :, None] * bb_re[None]
    ktap = (jnp.einsum('gcp,tgpd->tgcd', cr, w_re, precision=HI)
            - jnp.einsum('gcp,tgpd->tgcd', ci, w_im, precision=HI))
    rows = []
    for s in range(n):
        cols = []
        for i in range(n):
            cols.append(jnp.transpose(ktap[i - s], (0, 2, 1)) if s <= i else jnp.zeros((G, SG, SG), F32))
        rows.append(jnp.concatenate(cols, -1))
    toep = jnp.concatenate(rows, 1)
    CT = jnp.concatenate([top, toep], 1)
    l8r, l8i = pw_re[n], pw_im[n]
    lamA = jnp.concatenate([l8r, l8r], -1).reshape(1, G * 2 * P)
    lamB = jnp.concatenate([-l8i, l8i], -1).reshape(1, G * 2 * P)
    dvec = jnp.tile(d_skip.astype(F32).reshape(G, 1, SG), (1, n, 1)).reshape(1, G * n * SG)
    return B8, CT, lamA, lamB, dvec


def _s5_kernel(up_ref, us_ref, h0_ref, b8_ref, ct_ref, la_ref, lb_ref, d_ref,
               zp_ref, zs_ref, hp_ref, hs_ref, v_scr, h_scr, *, gb, bp):
    W = LANES
    rows_p = up_ref.shape[0]
    half = W // 2

    def cmul(h):
        sw = [pltpu.roll(h[:, g * W:(g + 1) * W], half, 1) for g in range(gb)]
        return la * h + lb * (jnp.concatenate(sw, 1) if gb > 1 else sw[0])

    la, lb = la_ref[...], lb_ref[...]
    for g in range(gb):
        sl = slice(g * W, (g + 1) * W)
        v_scr[:, sl] = jnp.dot(up_ref[:, sl].astype(BF16), b8_ref[g].astype(BF16),
                               preferred_element_type=F32)
    row = lax.broadcasted_iota(jnp.int32, (SUBLANES, gb * W), 0)
    lo_rows = row < bp

    def step(k2, c8):
        r0 = pl.multiple_of(k2 * SUBLANES, SUBLANES)
        x = v_scr[pl.ds(r0, SUBLANES), :]
        t1 = cmul(c8) + x
        r = pltpu.roll(t1, bp, 0)
        h_scr[pl.ds(r0, SUBLANES), :] = jnp.where(lo_rows, c8, r)
        t2 = cmul(r) + x
        return pltpu.roll(t2, bp, 0)

    c_end = lax.fori_loop(0, rows_p // SUBLANES, step, jnp.zeros((SUBLANES, gb * W), F32))
    hp_ref[...] = c_end
    for g in range(gb):
        sl = slice(g * W, (g + 1) * W)
        u = up_ref[:, sl]
        y = (jnp.dot(h_scr[:, sl].astype(BF16), ct_ref[g, :W, :].astype(BF16), preferred_element_type=F32)
             + jnp.dot(u.astype(BF16), ct_ref[g, W:, :].astype(BF16), preferred_element_type=F32))
        zp_ref[:, sl] = jax.nn.gelu(y + d_ref[:, sl] * u)
    h0 = h0_ref[...]
    hnew = cmul(h0)
    for g in range(gb):
        sl = slice(g * W, (g + 1) * W)
        u = us_ref[:, sl]
        ub = u.astype(BF16)
        hs_ref[:, sl] = hnew[:, sl] + jnp.dot(ub, b8_ref[g].astype(BF16), preferred_element_type=F32)
        y = (jnp.dot(h0[:, sl].astype(BF16), ct_ref[g, :W, :].astype(BF16), preferred_element_type=F32)
             + jnp.dot(ub, ct_ref[g, W:, :].astype(BF16), preferred_element_type=F32))
        zs_ref[:, sl] = jax.nn.gelu(y + d_ref[:, sl] * u)


def _s5_mixer(h, n_p, shape_p, shape_s, state_s, prm):
    B8, CT, lamA, lamB, dvec = prm
    G = B8.shape[0]
    SG = LANES // S5_STEPS
    P = LANES // 2
    bp, lp = shape_p
    bs, ls = shape_s
    assert ls == S5_STEPS and lp % (2 * S5_STEPS) == 0 and 2 * bp == SUBLANES and bs % SUBLANES == 0
    D = h.shape[1]
    nk = lp // S5_STEPS
    up = h[:n_p].reshape(bp, nk, S5_STEPS, G, SG).transpose(1, 0, 3, 2, 4).reshape(nk * bp, G * LANES)
    us = h[n_p:].reshape(bs, S5_STEPS, G, SG).transpose(0, 2, 1, 3).reshape(bs, G * LANES)
    h0 = state_s.astype(F32).transpose(0, 1, 3, 2).reshape(bs, G * LANES)
    gb = _pick(G, (8, 4, 2, 1))
    wb = gb * LANES
    colp = pl.BlockSpec((nk * bp, wb), lambda j: (0, j))
    cols = pl.BlockSpec((bs, wb), lambda j: (0, j))
    vec = pl.BlockSpec((1, wb), lambda j: (0, j))
    zp, zs, hp, hs = pl.pallas_call(
        functools.partial(_s5_kernel, gb=gb, bp=bp),
        grid=(G // gb,),
        in_specs=[colp, cols, cols,
                  pl.BlockSpec((gb, LANES, LANES), lambda j: (j, 0, 0)),
                  pl.BlockSpec((gb, 2 * LANES, LANES), lambda j: (j, 0, 0)),
                  vec, vec, vec],
        out_specs=[colp, cols, pl.BlockSpec((SUBLANES, wb), lambda j: (0, j)), cols],
        out_shape=[jax.ShapeDtypeStruct((nk * bp, G * LANES), F32),
                   jax.ShapeDtypeStruct((bs, G * LANES), F32),
                   jax.ShapeDtypeStruct((SUBLANES, G * LANES), F32),
                   jax.ShapeDtypeStruct((bs, G * LANES), F32)],
        scratch_shapes=[pltpu.VMEM((nk * bp, wb), F32), pltpu.VMEM((nk * bp, wb), F32)],
        compiler_params=_cparams(("parallel",)))(up, us, h0, B8, CT, lamA, lamB, dvec)
    z_p = zp.reshape(nk, bp, G, S5_STEPS, SG).transpose(1, 0, 3, 2, 4).reshape(n_p, D)
    z_s = zs.reshape(bs, G, S5_STEPS, SG).transpose(0, 2, 1, 3).reshape(bs * ls, D)
    z = jnp.concatenate([z_p, z_s], 0)
    st = lambda a, nb: a.reshape(nb, G, 2, P).transpose(0, 1, 3, 2)
    return z, st(hp[:bp], bp), st(hs, bs)


def _conv_kernel(x_ref, halo_ref, buf_ref, cw_ref, o_ref, *, seq_rows, n_norm_q, n_norm_k, q_scale, first_axis):
    x = x_ref[...]
    R, CB = x.shape
    width = cw_ref.shape[0]
    row = lax.broadcasted_iota(jnp.int32, (R, CB), 0)
    if seq_rows == SUBLANES:
        prev = buf_ref[...]
        trow = row % SUBLANES
    else:
        prev = jnp.where(pl.program_id(first_axis) == 0, buf_ref[...], halo_ref[...])
        trow = row
    acc = x * cw_ref[width - 1:width, :]
    for j in range(1, width):
        s = pltpu.roll(x, j, 0)
        if seq_rows == SUBLANES:
            p = pltpu.roll(prev, (R - SUBLANES + j) % R, 0)
            s = jnp.where(trow < j, p, s)
        else:
            p = pltpu.roll(prev, j, 0)
            row8 = lax.broadcasted_iota(jnp.int32, (SUBLANES, CB), 0)
            top = jnp.where(row8 < j, p, s[:SUBLANES])
            s = jnp.concatenate([top, s[SUBLANES:]], 0) if R > SUBLANES else top
        acc = acc + s * cw_ref[width - 1 - j:width - j, :]
    c = acc * jax.nn.sigmoid(acc)
    cb = pl.program_id(2 if seq_rows != SUBLANES else 1)
    segs = []
    for hh in range(CB // LANES):
        seg = c[:, hh * LANES:(hh + 1) * LANES]
        segs.append(seg * lax.rsqrt(jnp.sum(seg * seg, axis=-1, keepdims=True) + 1e-6))
    normed = jnp.concatenate(segs, -1) if len(segs) > 1 else segs[0]
    scale = jnp.where(cb < n_norm_q, jnp.float32(q_scale), jnp.float32(1.0))
    o_ref[...] = jnp.where(cb < n_norm_q + n_norm_k, normed * scale, c)


def _conv_qkv(proj, conv_buf, conv_w, bt, L, conv_dim, key_dim, head_dim):
    width = conv_w.shape[0]
    assert width - 1 <= SUBLANES and L % SUBLANES == 0
    CB = _pick(key_dim, (512, 256, 128))
    ncb = conv_dim // CB
    nq = nk = key_dim // CB
    buf8 = jnp.concatenate([jnp.zeros((bt, SUBLANES - (width - 1), conv_dim), F32), conv_buf.astype(F32)], 1)
    buf8 = buf8.reshape(bt * SUBLANES, conv_dim)
    cw = conv_w.astype(F32)
    q_scale = float(head_dim) ** -0.5
    if L == SUBLANES:
        R = _pick(bt * L, (128, 64, 32, 16, 8))
        blk = pl.BlockSpec((R, CB), lambda i, c: (i, c))
        return pl.pallas_call(
            functools.partial(_conv_kernel, seq_rows=SUBLANES, n_norm_q=nq, n_norm_k=nk, q_scale=q_scale,
                              first_axis=0),
            grid=(bt * L // R, ncb),
            in_specs=[blk, blk, blk, pl.BlockSpec((width, CB), lambda i, c: (0, c))],
            out_specs=blk, out_shape=jax.ShapeDtypeStruct((bt * L, conv_dim), F32),
            compiler_params=_cparams(("parallel", "parallel")))(proj, buf8, buf8, cw)
    R = _pick(L, (256, 128, 64, 32, 16, 8))
    nt = L // R
    blk = pl.BlockSpec((R, CB), lambda b, t, c: (b * nt + t, c))
    halo = pl.BlockSpec((SUBLANES, CB),
                        lambda b, t, c: (jnp.maximum((b * nt + t) * (R // SUBLANES) - 1, 0), c))
    bufs = pl.BlockSpec((SUBLANES, CB), lambda b, t, c: (b, c))
    return pl.pallas_call(
        functools.partial(_conv_kernel, seq_rows=R, n_norm_q=nq, n_norm_k=nk, q_scale=q_scale, first_axis=1),
        grid=(bt, nt, ncb),
        in_specs=[blk, halo, bufs, pl.BlockSpec((width, CB), lambda b, t, c: (0, c))],
        out_specs=blk, out_shape=jax.ShapeDtypeStruct((bt * L, conv_dim), F32),
        compiler_params=_cparams(("parallel", "parallel", "parallel")))(proj, proj, buf8, cw)


def _dot_nt(a, b):
    return lax.dot_general(a.astype(BF16), b.astype(BF16), (((1,), (1,)), ((), ())), preferred_element_type=F32)


def _dot_tn(a, b):
    return lax.dot_general(a.astype(BF16), b.astype(BF16), (((0,), (0,)), ((), ())), preferred_element_type=F32)


def _dot(a, b):
    return jnp.dot(a.astype(BF16), b.astype(BF16), preferred_element_type=F32)


def _delta_kernel(q_ref, k_ref, v_ref, z_ref, col_ref, rowp_ref, s0_ref, ng_ref, o_ref, so_ref, s_scr,
                  *, hpg, C):
    n = pl.program_id(2)

    @pl.when(n == 0)
    def _():
        s_scr[...] = s0_ref[0]

    ri = lax.broadcasted_iota(jnp.int32, (C, C), 0)
    ci = lax.broadcasted_iota(jnp.int32, (C, C), 1)
    incl = ri >= ci
    strict = ri > ci
    eye = (ri == ci).astype(F32)
    col = col_ref[0, 0]
    rowp = rowp_ref[0, 0, 0]
    ng = ng_ref[...]
    rep = hpg // (q_ref.shape[-1] // LANES)
    for hh in range(hpg):
        qh = q_ref[0][:, (hh // rep) * LANES:(hh // rep + 1) * LANES]
        kh = k_ref[0][:, (hh // rep) * LANES:(hh // rep + 1) * LANES]
        vh = v_ref[0][:, hh * LANES:(hh + 1) * LANES]
        zh = z_ref[0][:, hh * LANES:(hh + 1) * LANES]
        gc = col[:, hh:hh + 1]
        beta = col[:, hpg + hh:hpg + hh + 1]
        gr = rowp[hh:hh + 1, :]
        S = s_scr[hh]
        eg = jnp.exp(gc)
        decay = jnp.where(incl, jnp.exp(jnp.where(incl, gc - gr, 0.0)), 0.0)
        kb = kh * beta
        m = jnp.where(strict, _dot_nt(kb, kh) * decay, 0.0)
        tinv = eye - jnp.where((ri // 2 == ci // 2), m, 0.0)
        s = 2
        while s < C:
            same = (ri // (2 * s)) == (ci // (2 * s))
            off = same & ((ri // s) % 2 == 1) & ((ci // s) % 2 == 0)
            tinv = tinv - _dot(_dot(tinv, jnp.where(off, m, 0.0)), tinv)
            s *= 2
        u = _dot(tinv, vh * beta)
        w = _dot(tinv, kb * eg)
        v_new = u - _dot(w, S)
        attn = jnp.where(incl, _dot_nt(qh, kh) * decay, 0.0)
        o = _dot(qh * eg, S) + _dot(attn, v_new)
        g_last = gc[C - 1:C, :]
        s_new = S * jnp.exp(g_last) + _dot_tn(kh * jnp.exp(g_last - gc), v_new)
        s_scr[hh] = s_new
        o = o * lax.rsqrt(jnp.mean(o * o, axis=-1, keepdims=True) + NORM_EPS) * ng
        o_ref[0, :, hh * LANES:(hh + 1) * LANES] = (o * (zh * jax.nn.sigmoid(zh))).astype(o_ref.dtype)

    @pl.when(n == pl.num_programs(2) - 1)
    def _():
        so_ref[0] = s_scr[...]


def _delta_rule(qkvc, proj, ba, s0, a_log, dt_bias, norm_g, bt, L, n_heads, key_dim, val_dim, chunk):
    H = n_heads
    C = chunk if L % chunk == 0 else L
    N = L // C
    hpg = 4
    HG = H // hpg
    conv_dim = 2 * key_dim + val_dim
    rep = H // (key_dim // LANES)
    qw = (hpg // rep) * LANES
    vw = hpg * LANES
    beta = jax.nn.sigmoid(ba[:, :H])
    g = -jnp.exp(a_log.astype(F32)) * jax.nn.softplus(ba[:, H:2 * H] + dt_bias.astype(F32))
    gcum = jnp.cumsum(g.reshape(bt, N, C, H), axis=2)
    col = jnp.concatenate([gcum.reshape(bt, L, HG, hpg), beta.reshape(bt, L, HG, hpg)], -1)
    col = col.transpose(0, 2, 1, 3)
    rowp = gcum.reshape(bt, N, C, HG, hpg).transpose(0, 3, 1, 4, 2)
    rowp = jnp.concatenate([rowp, jnp.zeros((bt, HG, N, SUBLANES - hpg, C), F32)], 3)
    q3 = qkvc.reshape(bt, L, conv_dim)
    p3 = proj.reshape(bt, L, proj.shape[1])
    kq0 = key_dim // qw
    v0 = 2 * key_dim // vw
    z0 = conv_dim // vw
    o, s_out = pl.pallas_call(
        functools.partial(_delta_kernel, hpg=hpg, C=C),
        grid=(bt, HG, N),
        in_specs=[pl.BlockSpec((1, C, qw), lambda b, h, n: (b, n, h)),
                  pl.BlockSpec((1, C, qw), lambda b, h, n: (b, n, kq0 + h)),
                  pl.BlockSpec((1, C, vw), lambda b, h, n: (b, n, v0 + h)),
                  pl.BlockSpec((1, C, vw), lambda b, h, n: (b, n, z0 + h)),
                  pl.BlockSpec((1, 1, C, 2 * hpg), lambda b, h, n: (b, h, n, 0)),
                  pl.BlockSpec((1, 1, 1, SUBLANES, C), lambda b, h, n: (b, h, n, 0, 0)),
                  pl.BlockSpec((1, hpg, LANES, LANES), lambda b, h, n: (b, h, 0, 0)),
                  pl.BlockSpec((1, LANES), lambda b, h, n: (0, 0))],
        out_specs=[pl.BlockSpec((1, C, vw), lambda b, h, n: (b, n, h)),
                   pl.BlockSpec((1, hpg, LANES, LANES), lambda b, h, n: (b, h, 0, 0))],
        out_shape=[jax.ShapeDtypeStruct((bt, L, val_dim), BF16 if C % 16 == 0 else F32),
                   jax.ShapeDtypeStruct((bt, H, LANES, LANES), F32)],
        scratch_shapes=[pltpu.VMEM((hpg, LANES, LANES), F32)],
        compiler_params=_cparams(("parallel", "parallel", "arbitrary")))(
            q3, q3, q3, p3, col, rowp, s0.astype(F32), norm_g.astype(F32).reshape(1, LANES))
    return o.reshape(bt * L, val_dim).astype(BF16), s_out


def _expert_kernel(be_ref, bv_ref, xs_ref, wg_ref, wu_ref, wd_ref, o_ref, acc_g, acc_u, hbuf, *, nkc):
    i = pl.program_id(0)
    c = pl.program_id(1)
    valid = bv_ref[i] > 0

    @pl.when(valid & (c < nkc))
    def _():
        xk = xs_ref[...]
        pg = jnp.dot(xk, wg_ref[0].astype(BF16), preferred_element_type=F32)
        pu = jnp.dot(xk, wu_ref[0].astype(BF16), preferred_element_type=F32)

        @pl.when(c == 0)
        def _():
            acc_g[...] = pg
            acc_u[...] = pu

        @pl.when(c > 0)
        def _():
            acc_g[...] += pg
            acc_u[...] += pu

        @pl.when(c == nkc - 1)
        def _():
            gte = acc_g[...]
            hbuf[...] = (gte * jax.nn.sigmoid(gte) * acc_u[...]).astype(BF16)

    @pl.when(valid & (c >= nkc))
    def _():
        o_ref[...] = jnp.dot(hbuf[...], wd_ref[0].astype(BF16), preferred_element_type=F32)

    @pl.when(jnp.logical_not(valid) & (c >= nkc))
    def _():
        o_ref[...] = jnp.zeros_like(o_ref)


def _experts(xs, blk_exp, blk_valid, w_gate, w_up, w_down):
    R, D = xs.shape
    E, _, FF = w_gate.shape
    NB = R // MOE_ROWS
    kc = _pick(D, (2048, 1024, 512, 256, 128))
    nkc = D // kc
    nc = kc
    nnc = D // nc

    def kidx(c, v):
        return jnp.where(v > 0, jnp.minimum(c, nkc - 1), nkc - 1)

    def nidx(c, v):
        return jnp.where(v > 0, jnp.maximum(c - nkc, 0), nnc - 1)

    grid_spec = pltpu.PrefetchScalarGridSpec(
        num_scalar_prefetch=2, grid=(NB, nkc + nnc),
        in_specs=[pl.BlockSpec((MOE_ROWS, kc), lambda i, c, be, bv: (i, kidx(c, bv[i]))),
                  pl.BlockSpec((1, kc, FF), lambda i, c, be, bv: (be[i], kidx(c, bv[i]), 0)),
                  pl.BlockSpec((1, kc, FF), lambda i, c, be, bv: (be[i], kidx(c, bv[i]), 0)),
                  pl.BlockSpec((1, FF, nc), lambda i, c, be, bv: (be[i], 0, nidx(c, bv[i])))],
        out_specs=pl.BlockSpec((MOE_ROWS, nc), lambda i, c, be, bv: (i, jnp.maximum(c - nkc, 0))),
        scratch_shapes=[pltpu.VMEM((MOE_ROWS, FF), F32), pltpu.VMEM((MOE_ROWS, FF), F32),
                        pltpu.VMEM((MOE_ROWS, FF), BF16)])
    return pl.pallas_call(
        functools.partial(_expert_kernel, nkc=nkc), grid_spec=grid_spec,
        out_shape=jax.ShapeDtypeStruct((R, D), F32),
        compiler_params=_cparams(("arbitrary", "arbitrary")))(blk_exp, blk_valid, xs, w_gate, w_up, w_down)


def _moe(h2, eidx, wts, w_gate, w_up, w_down):
    T, D = h2.shape
    E = w_gate.shape[0]
    A = T * MOE_TOPK
    BM = MOE_ROWS
    flat_e = eidx.reshape(A)
    order = jnp.argsort(flat_e)
    e_sorted = flat_e[order]
    counts = jnp.zeros((E,), jnp.int32).at[flat_e].add(1)
    padded = (counts + BM - 1) // BM * BM
    pad_end = jnp.cumsum(padded)
    pad_start = pad_end - padded
    start = jnp.cumsum(counts) - counts
    dest = pad_start[e_sorted] + jnp.arange(A, dtype=jnp.int32) - start[e_sorted]
    NB = (A + E * (BM - 1)) // BM
    R = NB * BM
    row_tok = jnp.zeros((R,), jnp.int32).at[dest].set((order // MOE_TOPK).astype(jnp.int32))
    pos = jnp.zeros((A,), jnp.int32).at[order].set(dest)
    blk_start = jnp.arange(NB, dtype=jnp.int32) * BM
    blk_valid = (blk_start < pad_end[-1]).astype(jnp.int32)
    last_used = jnp.maximum(pad_end[-1] // BM - 1, 0)
    blk_exp = jnp.minimum(jnp.searchsorted(pad_end, jnp.minimum(blk_start, last_used * BM), side='right'),
                          E - 1).astype(jnp.int32)
    xs = jnp.take(h2, row_tok, axis=0)
    ys = _experts(xs, blk_exp, blk_valid, w_gate, w_up, w_down)
    yk = jnp.take(ys, pos, axis=0).reshape(T, MOE_TOPK, D)
    return jnp.sum(yk * wts[:, :, None], axis=1)


def kernel(x_prompt, x_sample, state_ssm, state_conv, state_gdn, attn_norm, ffn_norm, final_norm,
           ssm_lambda_re, ssm_lambda_im, ssm_log_dt, ssm_b_re, ssm_b_im, ssm_c_re, ssm_c_im, ssm_d,
           ssm_w_glu, ssm_b_glu, gdn_w_in, gdn_conv_w, gdn_a_log, gdn_dt_bias, gdn_norm, gdn_w_out,
           moe_w_grp, moe_b_grp, moe_w_exp, moe_b_exp, moe_w_gate, moe_w_up, moe_w_down):
    bp, lp, D = x_prompt.shape
    bs, ls, _ = x_sample.shape
    n_p, n_s = bp * lp, bs * ls
    depth = attn_norm.shape[0]
    n_mix = 2
    H = gdn_a_log.shape[1]
    head_dim = gdn_norm.shape[1]
    val_dim = H * head_dim
    conv_dim = gdn_conv_w.shape[2]
    key_dim = (conv_dim - val_dim) // 2
    n_grp = moe_w_grp.shape[2]
    n_exp = moe_w_exp.shape[2]
    n_per = n_exp // n_grp
    assert head_dim == LANES and n_grp + n_exp <= LANES

    x = jnp.concatenate([x_prompt.reshape(n_p, D), x_sample.reshape(n_s, D)], 0).astype(F32)
    delta = None
    ssm_p, ssm_s, conv_p, conv_s, gdn_p, gdn_s = [], [], [], [], [], []
    for i in range(depth):
        j = i // n_mix
        if i % n_mix == 0:
            x, h = _add_norm(x, delta, attn_norm[i], F32)
            prm = _s5_params(ssm_lambda_re[j], ssm_lambda_im[j], ssm_log_dt[j], ssm_b_re[j], ssm_b_im[j],
                             ssm_c_re[j], ssm_c_im[j], ssm_d[j])
            z, hp_new, hs_new = _s5_mixer(h, n_p, (bp, lp), (bs, ls), state_ssm[j], prm)
            ssm_p.append(hp_new.astype(x.dtype))
            ssm_s.append(hs_new.astype(x.dtype))
            x = _matmul(z.astype(BF16), ssm_w_glu[j], D, mode="glu", res=x, z=z, bias=ssm_b_glu[j].astype(F32))
        else:
            x, h = _add_norm(x, delta, attn_norm[i], BF16)
            w_in = gdn_w_in[j]
            n_main = conv_dim + val_dim
            proj = _matmul(h, w_in, n_main)
            w_ba = jnp.concatenate([w_in[:, n_main:], jnp.zeros((D, LANES - 2 * H), F32)], 1)
            ba = _matmul(h, w_ba, LANES, tn=LANES)
            outs = []
            for (r0, bt, L, buf, s0, cl, sl) in (
                    (0, bp, lp, jnp.zeros((bp,) + state_conv.shape[2:], F32),
                     jnp.zeros((bp,) + state_gdn.shape[2:], F32), conv_p, gdn_p),
                    (n_p, bs, ls, state_conv[j], state_gdn[j], conv_s, gdn_s)):
                pj = proj[r0:r0 + bt * L]
                qkvc = _conv_qkv(pj, buf, gdn_conv_w[j], bt, L, conv_dim, key_dim, head_dim)
                o, s_new = _delta_rule(qkvc, pj, ba[r0:r0 + bt * L], s0, gdn_a_log[j], gdn_dt_bias[j],
                                       gdn_norm[j], bt, L, H, key_dim, val_dim, 64)
                outs.append(o)
                width = gdn_conv_w.shape[1]
                cl.append(pj.reshape(bt, L, n_main)[:, L - (width - 1):, :conv_dim])
                sl.append(s_new)
            x = _matmul(jnp.concatenate(outs, 0), gdn_w_out[j], D, mode="res", res=x)
        wr = jnp.concatenate([moe_w_grp[i], moe_w_exp[i],
                              jnp.zeros((D, LANES - n_grp - n_exp), F32)], 1)
        br = jnp.concatenate([moe_b_grp[i], moe_b_exp[i], jnp.zeros((LANES - n_grp - n_exp,), F32)])
        h2, eidx, wts = _add_norm(x, None, ffn_norm[i], BF16, router=(wr, br.reshape(1, LANES), n_grp, n_per))[1:]
        delta = _moe(h2, eidx[:, :MOE_TOPK], wts[:, :MOE_TOPK], moe_w_gate[i], moe_w_up[i], moe_w_down[i])
    x, y = _add_norm(x, delta, final_norm, F32)
    return (y[:n_p].reshape(bp, lp, D), y[n_p:].reshape(bs, ls, D),
            jnp.stack(ssm_p), jnp.stack(conv_p), jnp.stack(gdn_p),
            jnp.stack(ssm_s), jnp.stack(conv_s), jnp.stack(gdn_s))
```

```python
import functools

import jax
import jax.numpy as jnp
from jax import lax
from jax.experimental import pallas as pl
from jax.experimental.pallas import tpu as pltpu

F32 = jnp.float32
BF16 = jnp.bfloat16
HI = lax.Precision.HIGHEST

LANES = 128
SUBLANES = 8
VMEM_LIMIT = 56 * 1024 * 1024
NORM_EPS = 1e-6
S5_STEPS = 8
MOE_ROWS = 512
MOE_TOPK = 2


def _cparams(sem):
    return pltpu.CompilerParams(dimension_semantics=sem, vmem_limit_bytes=VMEM_LIMIT)


def _pick(n, cands):
    for c in cands:
        if n % c == 0:
            return c
    return n


def _row_copy(src_hbm, src_row, dst_vmem, dst_row, sem):
    return pltpu.make_async_copy(src_hbm.at[pl.ds(src_row, 1), :], dst_vmem.at[pl.ds(dst_row, 1), :], sem)


def _norm_kernel(*refs, has_moe, has_router, n_grp, n_per):
    it = iter(refs)
    pos_ref = next(it) if has_moe else None
    x_ref = next(it)
    if has_moe:
        cw_ref, ys_hbm = next(it), next(it)
    g_ref = next(it)
    if has_router:
        wr_ref, br_ref = next(it), next(it)
    xo_ref = next(it) if has_moe else None
    h_ref = next(it)
    if has_router:
        idx_ref, wts_ref = next(it), next(it)
    if has_moe:
        yg, sem = next(it), next(it)

    x = x_ref[...]
    if has_moe:
        tm = x.shape[0]
        base = pl.program_id(0) * tm

        def start(r, carry):
            for k in range(MOE_TOPK):
                _row_copy(ys_hbm, pos_ref[(base + r) * MOE_TOPK + k], yg, k * tm + r, sem).start()
            return carry

        def wait(r, carry):
            _row_copy(ys_hbm, 0, yg, r, sem).wait()
            return carry

        lax.fori_loop(0, tm, start, 0)
        lax.fori_loop(0, MOE_TOPK * tm, wait, 0)
        cw = cw_ref[...]
        moe = yg[0:tm, :] * cw[:, 0:1]
        for k in range(1, MOE_TOPK):
            moe = moe + yg[k * tm:(k + 1) * tm, :] * cw[:, k:k + 1]
        x = x + moe
        xo_ref[...] = x
    h = x * lax.rsqrt(jnp.mean(x * x, axis=-1, keepdims=True) + NORM_EPS) * g_ref[...]
    h_ref[...] = h.astype(h_ref.dtype)
    if not has_router:
        return

    logits = jnp.dot(h, wr_ref[...], precision=HI, preferred_element_type=F32) + br_ref[...]
    lane = lax.broadcasted_iota(jnp.int32, logits.shape, 1).astype(F32)
    neg = jnp.float32(-jnp.inf)
    big = jnp.float32(2 ** 20)
    is_grp = lane < n_grp
    lg = jnp.where(is_grp, logits, neg)
    gmax = jnp.max(lg, axis=-1, keepdims=True)
    gsel = jnp.min(jnp.where(lg == gmax, lane, big), axis=-1, keepdims=True)
    pg = 1.0 / jnp.sum(jnp.where(is_grp, jnp.exp(lg - gmax), 0.0), axis=-1, keepdims=True)
    lo = n_grp + gsel * n_per
    in_sel = (lane >= lo) & (lane < lo + n_per)
    le = jnp.where(in_sel, logits, neg)
    m1 = jnp.max(le, axis=-1, keepdims=True)
    i1 = jnp.min(jnp.where(le == m1, lane, big), axis=-1, keepdims=True)
    le2 = jnp.where(lane == i1, neg, le)
    m2 = jnp.max(le2, axis=-1, keepdims=True)
    i2 = jnp.min(jnp.where(le2 == m2, lane, big), axis=-1, keepdims=True)
    e2 = jnp.exp(m2 - m1)
    w1 = pg / (1.0 + e2)
    w2 = pg * e2 / (1.0 + e2)
    idx_ref[...] = jnp.where(lane == 0, i1 - n_grp, jnp.where(lane == 1, i2 - n_grp, 0.0)).astype(jnp.int32)
    wts_ref[...] = jnp.where(lane == 0, w1, jnp.where(lane == 1, w2, 0.0))


def _add_norm(x, moe, g, h_dtype, router=None):
    T, D = x.shape
    has_moe = moe is not None
    tm = _pick(T, (128, 64, 32, 16, 8)) if has_moe else _pick(T, (256, 128, 64, 32, 16, 8))
    row = pl.BlockSpec((tm, D), lambda i, *_: (i, 0))
    small = pl.BlockSpec((tm, LANES), lambda i, *_: (i, 0))
    full = lambda a: pl.BlockSpec(a.shape, lambda i, *_: (0,) * a.ndim)
    ins, specs, prefetch, scratch = [x], [row], [], []
    if has_moe:
        ys, pos, cw = moe
        prefetch = [pos]
        ins += [cw, ys]
        specs += [small, pl.BlockSpec(memory_space=pl.ANY)]
        scratch = [pltpu.VMEM((MOE_TOPK * tm, D), F32), pltpu.SemaphoreType.DMA]
    g2 = g.reshape(1, D).astype(F32)
    ins.append(g2)
    specs.append(full(g2))
    outs, ospecs = [], []
    if has_moe:
        outs.append(jax.ShapeDtypeStruct((T, D), F32))
        ospecs.append(row)
    outs.append(jax.ShapeDtypeStruct((T, D), h_dtype))
    ospecs.append(row)
    n_grp = n_per = 0
    if router is not None:
        wr, br, n_grp, n_per = router
        ins += [wr, br]
        specs += [full(wr), full(br)]
        outs += [jax.ShapeDtypeStruct((T, LANES), jnp.int32), jax.ShapeDtypeStruct((T, LANES), F32)]
        ospecs += [small, small]
    grid_spec = pltpu.PrefetchScalarGridSpec(
        num_scalar_prefetch=len(prefetch), grid=(T // tm,), in_specs=specs, out_specs=ospecs,
        scratch_shapes=scratch)
    res = pl.pallas_call(
        functools.partial(_norm_kernel, has_moe=has_moe, has_router=router is not None,
                          n_grp=n_grp, n_per=n_per),
        grid_spec=grid_spec, out_shape=outs,
        compiler_params=_cparams(("arbitrary",) if has_moe else ("parallel",)))(*prefetch, *ins)
    res = list(res)
    x_new = res.pop(0) if has_moe else x
    return (x_new, *res)


def _mm_kernel(*refs, mode):
    if mode == "plain":
        a_ref, w_ref, o_ref, wbf = refs
    elif mode == "res":
        a_ref, w_ref, r_ref, o_ref, wbf = refs
    else:
        a_ref, w_ref, r_ref, z_ref, b_ref, o_ref, wbf = refs

    @pl.when(pl.program_id(1) == 0)
    def _():
        wbf[...] = w_ref[...].astype(BF16)

    acc = jnp.dot(a_ref[...], wbf[...], preferred_element_type=F32)
    if mode == "plain":
        o_ref[...] = acc.astype(o_ref.dtype)
    elif mode == "res":
        o_ref[...] = r_ref[...] + acc
    else:
        o_ref[...] = r_ref[...] + z_ref[...] * jax.nn.sigmoid(acc + b_ref[...])


def _matmul(a, w, n_cols, mode="plain", col0=0, tn=512, res=None, z=None, bias=None, out_dtype=F32):
    M, K = a.shape
    tn = min(tn, n_cols)
    assert n_cols % tn == 0 and col0 % tn == 0
    tm = _pick(M, (1024, 512, 256, 128, 64, 32, 16))
    nb0 = col0 // tn
    a_spec = pl.BlockSpec((tm, K), lambda j, i: (i, 0))
    w_spec = pl.BlockSpec((K, tn), lambda j, i: (0, j + nb0))
    o_spec = pl.BlockSpec((tm, tn), lambda j, i: (i, j))
    ins, specs = [a, w], [a_spec, w_spec]
    if mode in ("res", "glu"):
        ins.append(res)
        specs.append(o_spec)
    if mode == "glu":
        ins += [z, bias.reshape(1, -1)]
        specs += [o_spec, pl.BlockSpec((1, tn), lambda j, i: (0, j))]
    return pl.pallas_call(
        functools.partial(_mm_kernel, mode=mode),
        grid=(n_cols // tn, M // tm), in_specs=specs, out_specs=o_spec,
        out_shape=jax.ShapeDtypeStruct((M, n_cols), out_dtype),
        scratch_shapes=[pltpu.VMEM((K, tn), BF16)],
        compiler_params=_cparams(("parallel", "arbitrary")))(*ins)


def _s5_params(lam_re, lam_im, log_dt, b_re, b_im, c_re, c_im, d_skip):
    G, P = lam_re.shape
    SG = b_re.shape[-1]
    n = S5_STEPS
    dt = jnp.exp(log_dt.astype(F32))[:, None]
    a, b = lam_re.astype(F32) * dt, lam_im.astype(F32) * dt
    lb_re, lb_im = jnp.exp(a) * jnp.cos(b), jnp.exp(a) * jnp.sin(b)
    nr, ni = lb_re - 1.0, lb_im
    den = lam_re * lam_re + lam_im * lam_im
    cf_re = (nr * lam_re + ni * lam_im) / den
    cf_im = (ni * lam_re - nr * lam_im) / den
    bb_re = cf_re[..., None] * b_re - cf_im[..., None] * b_im
    bb_im = cf_re[..., None] * b_im + cf_im[..., None] * b_re
    ks = jnp.arange(n + 1, dtype=F32)[:, None, None]
    pw_re = jnp.exp(ks * a) * jnp.cos(ks * b)
    pw_im = jnp.exp(ks * a) * jnp.sin(ks * b)
    pr, pi = pw_re[n - 1::-1][:n], pw_im[n - 1::-1][:n]
    t_re = pr[..., None] * bb_re[None] - pi[..., None] * bb_im[None]
    t_im = pr[..., None] * bb_im[None] + pi[..., None] * bb_re[None]
    B8 = jnp.concatenate([jnp.transpose(t_re, (1, 0, 3, 2)), jnp.transpose(t_im, (1, 0, 3, 2))], -1)
    B8 = B8.reshape(G, n * SG, 2 * P)
    cr, ci = c_re.astype(F32), c_im.astype(F32)
    qr, qi = pw_re[1:], pw_im[1:]
    m_re = cr[None] * qr[:, :, None, :] - ci[None] * qi[:, :, None, :]
    m_im = cr[None] * qi[:, :, None, :] + ci[None] * qr[:, :, None, :]
    top = jnp.concatenate([jnp.transpose(m_re, (1, 3, 0, 2)), -jnp.transpose(m_im, (1, 3, 0, 2))], 1)
    top = top.reshape(G, 2 * P, n * SG)
    w_re = pw_re[:n, :, :, None] * bb_re[None] - pw_im[:n, :, :, None] * bb_im[None]
    w_im = pw_re[:n, :, :, None] * bb_im[None] + pw_im[:n, :, :, None] * bb_re[None]
    ktap = (jnp.einsum('gcp,tgpd->tgcd', cr, w_re, precision=HI)
            - jnp.einsum('gcp,tgpd->tgcd', ci, w_im, precision=HI))
    rows = []
    for s in range(n):
        cols = []
        for i in range(n):
            cols.append(jnp.transpose(ktap[i - s], (0, 2, 1)) if s <= i else jnp.zeros((G, SG, SG), F32))
        rows.append(jnp.concatenate(cols, -1))
    toep = jnp.concatenate(rows, 1)
    CT = jnp.concatenate([top, toep], 1)
    z64 = jnp.zeros((G // 2, n * SG, P), F32)
    e, o = B8[0::2], B8[1::2]
    B8p = jnp.concatenate([
        jnp.concatenate([e[..., :P], z64, e[..., P:], z64], -1),
        jnp.concatenate([z64, o[..., :P], z64, o[..., P:]], -1)], 1)
    ce, co = CT[0::2], CT[1::2]
    zc = lambda r: jnp.zeros((G // 2, r, n * SG), F32)
    CTp = jnp.concatenate([
        jnp.concatenate([ce[:, :P], zc(P)], -1), jnp.concatenate([zc(P), co[:, :P]], -1),
        jnp.concatenate([ce[:, P:2 * P], zc(P)], -1), jnp.concatenate([zc(P), co[:, P:2 * P]], -1),
        jnp.concatenate([ce[:, 2 * P:], zc(n * SG)], -1), jnp.concatenate([zc(n * SG), co[:, 2 * P:]], -1)], 1)
    l8r, l8i = pw_re[n].reshape(1, G * P), pw_im[n].reshape(1, G * P)
    return B8p, CTp, l8r, l8i, d_skip.astype(F32).reshape(1, G * SG)


def _s5_kernel(x_ref, d_ref, h0r_ref, h0i_ref, b8_ref, ct_ref, lr_ref, li_ref,
               z_ref, hpr_ref, hpi_ref, hsr_ref, hsi_ref, u_scr, v_scr, h_scr, y_scr, *, bp, nk, bs):
    W = LANES
    n = S5_STEPS
    SG = W // n
    rp = bp * nk
    ra = rp + bs
    rc_rows = _pick(ra, (64, 32, 16, 8))
    chunk = lax.broadcasted_iota(jnp.int32, (rc_rows, W), 1) // SG

    def chunk_transpose(vs):
        vs = list(vs)
        d = 1
        while d < n:
            hi_bit = (chunk // d) % 2 == 1
            for i in range(n):
                if (i // d) % 2 == 0:
                    a, b = vs[i], vs[i + d]
                    vs[i] = jnp.where(hi_bit, pltpu.roll(b, d * SG, 1), a)
                    vs[i + d] = jnp.where(hi_bit, b, pltpu.roll(a, W - d * SG, 1))
            d *= 2
        return vs

    def relayout_in(rc, c):
        r0 = pl.multiple_of(rc * rc_rows, rc_rows)
        us = chunk_transpose([x_ref[pl.ds(r0 * n + i, rc_rows, stride=n), :] for i in range(n)])
        for g in range(n):
            u_scr[g, pl.ds(r0, rc_rows), :] = us[g]
        return c

    lax.fori_loop(0, ra // rc_rows, relayout_in, 0)

    npair = n // 2
    for p in range(npair):
        ub = jnp.concatenate([u_scr[2 * p], u_scr[2 * p + 1]], 1).astype(BF16)
        v = jnp.dot(ub, b8_ref[p].astype(BF16), preferred_element_type=F32)
        v_scr[2 * p] = v[:rp, :W]
        v_scr[2 * p + 1] = v[:rp, W:]
        sl = slice(p * W, (p + 1) * W)
        h0r, h0i = h0r_ref[:, sl], h0i_ref[:, sl]
        lr, li = lr_ref[:, sl], li_ref[:, sl]
        h_scr[2 * p, rp:, :] = h0r
        h_scr[2 * p + 1, rp:, :] = h0i
        hsr_ref[:, sl] = lr * h0r - li * h0i + v[rp:, :W]
        hsi_ref[:, sl] = lr * h0i + li * h0r + v[rp:, W:]

    lrs = [lr_ref[:, p * W:(p + 1) * W] for p in range(npair)]
    lis = [li_ref[:, p * W:(p + 1) * W] for p in range(npair)]

    def step(k, carry):
        out = []
        for p in range(npair):
            hr, hi = carry[2 * p], carry[2 * p + 1]
            h_scr[2 * p, pl.ds(k, bp, stride=nk), :] = hr
            h_scr[2 * p + 1, pl.ds(k, bp, stride=nk), :] = hi
            vr = v_scr[2 * p, pl.ds(k, bp, stride=nk), :]
            vi = v_scr[2 * p + 1, pl.ds(k, bp, stride=nk), :]
            out.append(lrs[p] * hr - lis[p] * hi + vr)
            out.append(lrs[p] * hi + lis[p] * hr + vi)
        return tuple(out)

    fin = lax.fori_loop(0, nk, step, tuple(jnp.zeros((bp, W), F32) for _ in range(n)))
    for p in range(npair):
        hpr_ref[:, p * W:(p + 1) * W] = fin[2 * p]
        hpi_ref[:, p * W:(p + 1) * W] = fin[2 * p + 1]

    for p in range(npair):
        lhs = jnp.concatenate([h_scr[2 * p], h_scr[2 * p + 1], u_scr[2 * p], u_scr[2 * p + 1]], 1)
        y = jnp.dot(lhs.astype(BF16), ct_ref[p].astype(BF16), preferred_element_type=F32)
        y_scr[2 * p] = y[:, :W]
        y_scr[2 * p + 1] = y[:, W:]

    dv = d_ref[...]

    def relayout_out(rc, c):
        r0 = pl.multiple_of(rc * rc_rows, rc_rows)
        zs = chunk_transpose([y_scr[g, pl.ds(r0, rc_rows), :] for g in range(n)])
        for i in range(n):
            a = x_ref[pl.ds(r0 * n + i, rc_rows, stride=n), :]
            z_ref[pl.ds(r0 * n + i, rc_rows, stride=n), :] = jax.nn.gelu(zs[i] + dv * a)
        return c

    lax.fori_loop(0, ra // rc_rows, relayout_out, 0)


def _s5_mixer(h, shape_p, shape_s, state_s, prm):
    B8p, CTp, l8r, l8i, dvec = prm
    T, D = h.shape
    G = 2 * B8p.shape[0]
    P = LANES // 2
    bp, lp = shape_p
    bs, ls = shape_s
    nk = lp // S5_STEPS
    ra = T // S5_STEPS
    assert ls == S5_STEPS and lp % S5_STEPS == 0 and ra % SUBLANES == 0 and bp * nk + bs == ra
    h0r = state_s[..., 0].astype(F32).reshape(bs, G * P)
    h0i = state_s[..., 1].astype(F32).reshape(bs, G * P)
    gpb = S5_STEPS
    sw = gpb * P
    xcol = pl.BlockSpec((T, LANES), lambda j: (0, j))
    st_s = pl.BlockSpec((bs, sw), lambda j: (0, j))
    st_p = pl.BlockSpec((bp, sw), lambda j: (0, j))
    vec = pl.BlockSpec((1, sw), lambda j: (0, j))
    z, hpr, hpi, hsr, hsi = pl.pallas_call(
        functools.partial(_s5_kernel, bp=bp, nk=nk, bs=bs),
        grid=(G // gpb,),
        in_specs=[xcol, pl.BlockSpec((1, LANES), lambda j: (0, j)), st_s, st_s,
                  pl.BlockSpec((gpb // 2, 2 * LANES, 2 * LANES), lambda j: (j, 0, 0)),
                  pl.BlockSpec((gpb // 2, 4 * LANES, 2 * LANES), lambda j: (j, 0, 0)),
                  vec, vec],
        out_specs=[xcol, st_p, st_p, st_s, st_s],
        out_shape=[jax.ShapeDtypeStruct((T, D), F32),
                   jax.ShapeDtypeStruct((bp, G * P), F32), jax.ShapeDtypeStruct((bp, G * P), F32),
                   jax.ShapeDtypeStruct((bs, G * P), F32), jax.ShapeDtypeStruct((bs, G * P), F32)],
        scratch_shapes=[pltpu.VMEM((gpb, ra, LANES), F32), pltpu.VMEM((gpb, bp * nk, LANES), F32),
                        pltpu.VMEM((gpb, ra, LANES), F32), pltpu.VMEM((gpb, ra, LANES), F32)],
        compiler_params=_cparams(("parallel",)))(h, dvec, h0r, h0i, B8p, CTp, l8r, l8i)
    st = lambda r, i, nb: jnp.stack([r.reshape(nb, G, P), i.reshape(nb, G, P)], -1)
    return z, st(hpr, hpi, bp), st(hsr, hsi, bs)


def _conv_kernel(x_ref, halo_ref, buf_ref, cw_ref, o_ref, *, seq_rows, n_norm_q, n_norm_k, q_scale, first_axis):
    x = x_ref[...]
    R, CB = x.shape
    width = cw_ref.shape[0]
    row = lax.broadcasted_iota(jnp.int32, (R, CB), 0)
    if seq_rows == SUBLANES:
        prev = buf_ref[...]
        trow = row % SUBLANES
    else:
        prev = jnp.where(pl.program_id(first_axis) == 0, buf_ref[...], halo_ref[...])
        trow = row
    acc = x * cw_ref[width - 1:width, :]
    for j in range(1, width):
        s = pltpu.roll(x, j, 0)
        if seq_rows == SUBLANES:
            p = pltpu.roll(prev, (R - SUBLANES + j) % R, 0)
            s = jnp.where(trow < j, p, s)
        else:
            p = pltpu.roll(prev, j, 0)
            row8 = lax.broadcasted_iota(jnp.int32, (SUBLANES, CB), 0)
            top = jnp.where(row8 < j, p, s[:SUBLANES])
            s = jnp.concatenate([top, s[SUBLANES:]], 0) if R > SUBLANES else top
        acc = acc + s * cw_ref[width - 1 - j:width - j, :]
    c = acc * jax.nn.sigmoid(acc)
    cb = pl.program_id(2 if seq_rows != SUBLANES else 1)
    segs = []
    for hh in range(CB // LANES):
        seg = c[:, hh * LANES:(hh + 1) * LANES]
        segs.append(seg * lax.rsqrt(jnp.sum(seg * seg, axis=-1, keepdims=True) + 1e-6))
    normed = jnp.concatenate(segs, -1) if len(segs) > 1 else segs[0]
    scale = jnp.where(cb < n_norm_q, jnp.float32(q_scale), jnp.float32(1.0))
    o_ref[...] = jnp.where(cb < n_norm_q + n_norm_k, normed * scale, c)


def _conv_qkv(proj, row0, conv_buf, conv_w, bt, L, conv_dim, key_dim, head_dim):
    width = conv_w.shape[0]
    assert width - 1 <= SUBLANES and L % SUBLANES == 0
    CB = _pick(key_dim, (512, 256, 128))
    ncb = conv_dim // CB
    nq = nk = key_dim // CB
    buf8 = jnp.concatenate([jnp.zeros((bt, SUBLANES - (width - 1), conv_dim), F32), conv_buf.astype(F32)], 1)
    buf8 = buf8.reshape(bt * SUBLANES, conv_dim)
    cw = conv_w.astype(F32)
    q_scale = float(head_dim) ** -0.5
    if L == SUBLANES:
        R = _pick(bt * L, (128, 64, 32, 16, 8))
        assert row0 % R == 0
        blk = pl.BlockSpec((R, CB), lambda i, c: (i, c))
        xin = pl.BlockSpec((R, CB), lambda i, c: (i + row0 // R, c))
        return pl.pallas_call(
            functools.partial(_conv_kernel, seq_rows=SUBLANES, n_norm_q=nq, n_norm_k=nk, q_scale=q_scale,
                              first_axis=0),
            grid=(bt * L // R, ncb),
            in_specs=[xin, blk, blk, pl.BlockSpec((width, CB), lambda i, c: (0, c))],
            out_specs=blk, out_shape=jax.ShapeDtypeStruct((bt * L, conv_dim), F32),
            compiler_params=_cparams(("parallel", "parallel")))(proj, buf8, buf8, cw)
    R = _pick(L, (256, 128, 64, 32, 16, 8))
    nt = L // R
    assert row0 % R == 0
    rb0 = row0 // R
    blk = pl.BlockSpec((R, CB), lambda b, t, c: (b * nt + t, c))
    xin = pl.BlockSpec((R, CB), lambda b, t, c: (rb0 + b * nt + t, c))
    halo = pl.BlockSpec((SUBLANES, CB),
                        lambda b, t, c: (jnp.maximum((rb0 + b * nt + t) * (R // SUBLANES) - 1, 0), c))
    bufs = pl.BlockSpec((SUBLANES, CB), lambda b, t, c: (b, c))
    return pl.pallas_call(
        functools.partial(_conv_kernel, seq_rows=R, n_norm_q=nq, n_norm_k=nk, q_scale=q_scale, first_axis=1),
        grid=(bt, nt, ncb),
        in_specs=[xin, halo, bufs, pl.BlockSpec((width, CB), lambda b, t, c: (0, c))],
        out_specs=blk, out_shape=jax.ShapeDtypeStruct((bt * L, conv_dim), F32),
        compiler_params=_cparams(("parallel", "parallel", "parallel")))(proj, proj, buf8, cw)


def _dot_nt(a, b):
    return lax.dot_general(a.astype(BF16), b.astype(BF16), (((1,), (1,)), ((), ())), preferred_element_type=F32)


def _dot_tn(a, b):
    return lax.dot_general(a.astype(BF16), b.astype(BF16), (((0,), (0,)), ((), ())), preferred_element_type=F32)


def _dot(a, b):
    return jnp.dot(a.astype(BF16), b.astype(BF16), preferred_element_type=F32)


def _delta_kernel(q_ref, k_ref, v_ref, z_ref, col_ref, rowp_ref, s0_ref, ng_ref, o_ref, so_ref, s_scr,
                  *, hpg, C):
    n = pl.program_id(2)

    @pl.when(n == 0)
    def _():
        s_scr[...] = s0_ref[0]

    ri = lax.broadcasted_iota(jnp.int32, (C, C), 0)
    ci = lax.broadcasted_iota(jnp.int32, (C, C), 1)
    incl = ri >= ci
    strict = ri > ci
    eye = (ri == ci).astype(F32)
    col = col_ref[0, 0]
    rowp = rowp_ref[0, 0, 0]
    ng = ng_ref[...]
    rep = hpg // (q_ref.shape[-1] // LANES)
    hs = range(hpg)
    lanes = lambda i: slice(i * LANES, (i + 1) * LANES)
    qs = [q_ref[:, lanes(i)] for i in range(hpg // rep)]
    ks = [k_ref[:, lanes(i)] for i in range(hpg // rep)]
    gc = [col[:, h:h + 1] for h in hs]
    beta = [col[:, hpg + h:hpg + h + 1] for h in hs]
    eg = [jnp.exp(gc[h]) for h in hs]
    decay = [jnp.where(incl, jnp.exp(jnp.where(incl, gc[h] - rowp[h:h + 1, :], 0.0)), 0.0) for h in hs]
    kb = [ks[h // rep] * beta[h] for h in hs]
    m = [jnp.where(strict, _dot_nt(kb[h], ks[h // rep]) * decay[h], 0.0) for h in hs]
    qk = [_dot_nt(qs[i], ks[i]) for i in range(hpg // rep)]
    blk2 = (ri // 2) == (ci // 2)
    tinv = [eye - jnp.where(blk2, m[h], 0.0) for h in hs]
    s = 2
    while s < C:
        off = ((ri // (2 * s)) == (ci // (2 * s))) & ((ri // s) % 2 == 1) & ((ci // s) % 2 == 0)
        tb = [_dot(tinv[h], jnp.where(off, m[h], 0.0)) for h in hs]
        tinv = [tinv[h] - _dot(tb[h], tinv[h]) for h in hs]
        s *= 2
    u = [_dot(tinv[h], v_ref[:, lanes(h)] * beta[h]) for h in hs]
    w = [_dot(tinv[h], kb[h] * eg[h]) for h in hs]
    S = [s_scr[h] for h in hs]
    ws = [_dot(jnp.concatenate([w[h], qs[h // rep] * eg[h]], 0), S[h]) for h in hs]
    v_new = [u[h] - ws[h][:C] for h in hs]
    o = [ws[h][C:] + _dot(jnp.where(incl, qk[h // rep] * decay[h], 0.0), v_new[h]) for h in hs]
    g_last = [gc[h][C - 1:C, :] for h in hs]
    s_new = [S[h] * jnp.exp(g_last[h]) + _dot_tn(ks[h // rep] * jnp.exp(g_last[h] - gc[h]), v_new[h]) for h in hs]
    for h in hs:
        s_scr[h] = s_new[h]
        zh = z_ref[:, lanes(h)]
        on = o[h] * lax.rsqrt(jnp.mean(o[h] * o[h], axis=-1, keepdims=True) + NORM_EPS) * ng
        o_ref[:, lanes(h)] = (on * (zh * jax.nn.sigmoid(zh))).astype(o_ref.dtype)

    @pl.when(n == pl.num_programs(2) - 1)
    def _():
        so_ref[0] = s_scr[...]


def _delta_rule(qkvc, proj, row0, ba, s0, a_log, dt_bias, norm_g, bt, L, n_heads, key_dim, val_dim, chunk, hpg):
    H = n_heads
    C = chunk if L % chunk == 0 else L
    N = L // C
    hpg = min(hpg, H)
    HG = H // hpg
    assert row0 % C == 0 and hpg % SUBLANES == 0
    conv_dim = 2 * key_dim + val_dim
    rep = H // (key_dim // LANES)
    qw = (hpg // rep) * LANES
    vw = hpg * LANES
    beta = jax.nn.sigmoid(ba[:, :H])
    g = -jnp.exp(a_log.astype(F32)) * jax.nn.softplus(ba[:, H:2 * H] + dt_bias.astype(F32))
    gcum = jnp.cumsum(g.reshape(bt, N, C, H), axis=2)
    col = jnp.concatenate([gcum.reshape(bt, L, HG, hpg), beta.reshape(bt, L, HG, hpg)], -1)
    col = col.transpose(0, 2, 1, 3)
    rowp = gcum.reshape(bt, N, C, HG, hpg).transpose(0, 3, 1, 4, 2)
    kq0 = key_dim // qw
    v0 = 2 * key_dim // vw
    z0 = conv_dim // vw
    rb0 = row0 // C
    o, s_out = pl.pallas_call(
        functools.partial(_delta_kernel, hpg=hpg, C=C),
        grid=(bt, HG, N),
        in_specs=[pl.BlockSpec((C, qw), lambda b, h, n: (b * N + n, h)),
                  pl.BlockSpec((C, qw), lambda b, h, n: (b * N + n, kq0 + h)),
                  pl.BlockSpec((C, vw), lambda b, h, n: (b * N + n, v0 + h)),
                  pl.BlockSpec((C, vw), lambda b, h, n: (rb0 + b * N + n, z0 + h)),
                  pl.BlockSpec((1, 1, C, 2 * hpg), lambda b, h, n: (b, h, n, 0)),
                  pl.BlockSpec((1, 1, 1, hpg, C), lambda b, h, n: (b, h, n, 0, 0)),
                  pl.BlockSpec((1, hpg, LANES, LANES), lambda b, h, n: (b, h, 0, 0)),
                  pl.BlockSpec((1, LANES), lambda b, h, n: (0, 0))],
        out_specs=[pl.BlockSpec((C, vw), lambda b, h, n: (b * N + n, h)),
                   pl.BlockSpec((1, hpg, LANES, LANES), lambda b, h, n: (b, h, 0, 0))],
        out_shape=[jax.ShapeDtypeStruct((bt * L, val_dim), BF16 if C % 16 == 0 else F32),
                   jax.ShapeDtypeStruct((bt, H, LANES, LANES), F32)],
        scratch_shapes=[pltpu.VMEM((hpg, LANES, LANES), F32)],
        compiler_params=_cparams(("parallel", "parallel", "arbitrary")))(
            qkvc, qkvc, qkvc, proj, col, rowp, s0.astype(F32), norm_g.astype(F32).reshape(1, LANES))
    return o.astype(BF16), s_out


def _expert_kernel(be_ref, bn_ref, tok_ref, x_hbm, wg_ref, wu_ref, wd_ref, o_ref, xg, acc_g, acc_u, hbuf, sem,
                   *, nkc):
    i = pl.program_id(0)
    c = pl.program_id(1)
    rows = bn_ref[i]
    valid = rows > 0
    bm = xg.shape[0]
    kc = wg_ref.shape[1]

    @pl.when((i == 0) & (c == 0))
    def _():
        xg[...] = jnp.zeros_like(xg)

    @pl.when(valid & (c == 0))
    def _():
        def start(r, carry):
            _row_copy(x_hbm, tok_ref[i * bm + r], xg, r, sem).start()
            return carry

        def wait(r, carry):
            _row_copy(x_hbm, 0, xg, r, sem).wait()
            return carry

        lax.fori_loop(0, rows, start, 0)
        lax.fori_loop(0, rows, wait, 0)

    for cc in range(nkc):
        @pl.when(valid & (c == cc))
        def _(cc=cc):
            xk = xg[:, cc * kc:(cc + 1) * kc].astype(BF16)
            pg = jnp.dot(xk, wg_ref[0].astype(BF16), preferred_element_type=F32)
            pu = jnp.dot(xk, wu_ref[0].astype(BF16), preferred_element_type=F32)
            if cc == 0:
                acc_g[...] = pg
                acc_u[...] = pu
            else:
                acc_g[...] += pg
                acc_u[...] += pu
            if cc == nkc - 1:
                gte = acc_g[...]
                hbuf[...] = (gte * jax.nn.sigmoid(gte) * acc_u[...]).astype(BF16)

    @pl.when(valid & (c >= nkc))
    def _():
        o_ref[...] = jnp.dot(hbuf[...], wd_ref[0].astype(BF16), preferred_element_type=F32)

    @pl.when(jnp.logical_not(valid) & (c >= nkc))
    def _():
        o_ref[...] = jnp.zeros_like(o_ref)


def _experts(x, row_tok, blk_exp, blk_rows, w_gate, w_up, w_down):
    T, D = x.shape
    R = row_tok.shape[0]
    E, _, FF = w_gate.shape
    NB = R // MOE_ROWS
    kc = _pick(D, (2048, 1024, 512, 256, 128))
    nkc = D // kc
    nc = kc
    nnc = D // nc

    def kidx(c, v):
        return jnp.where(v > 0, jnp.minimum(c, nkc - 1), nkc - 1)

    def nidx(c, v):
        return jnp.where(v > 0, jnp.maximum(c - nkc, 0), nnc - 1)

    grid_spec = pltpu.PrefetchScalarGridSpec(
        num_scalar_prefetch=3, grid=(NB, nkc + nnc),
        in_specs=[pl.BlockSpec(memory_space=pl.ANY),
                  pl.BlockSpec((1, kc, FF), lambda i, c, be, bn, tk: (be[i], kidx(c, bn[i]), 0)),
                  pl.BlockSpec((1, kc, FF), lambda i, c, be, bn, tk: (be[i], kidx(c, bn[i]), 0)),
                  pl.BlockSpec((1, FF, nc), lambda i, c, be, bn, tk: (be[i], 0, nidx(c, bn[i])))],
        out_specs=pl.BlockSpec((MOE_ROWS, nc), lambda i, c, be, bn, tk: (i, jnp.maximum(c - nkc, 0))),
        scratch_shapes=[pltpu.VMEM((MOE_ROWS, D), F32),
                        pltpu.VMEM((MOE_ROWS, FF), F32), pltpu.VMEM((MOE_ROWS, FF), F32),
                        pltpu.VMEM((MOE_ROWS, FF), BF16), pltpu.SemaphoreType.DMA])
    return pl.pallas_call(
        functools.partial(_expert_kernel, nkc=nkc), grid_spec=grid_spec,
        out_shape=jax.ShapeDtypeStruct((R, D), F32),
        compiler_params=_cparams(("arbitrary", "arbitrary")))(blk_exp, blk_rows, row_tok, x, w_gate, w_up, w_down)


def _moe(h2, eidx, w_gate, w_up, w_down):
    T, D = h2.shape
    E = w_gate.shape[0]
    A = T * MOE_TOPK
    BM = MOE_ROWS
    flat_e = eidx.reshape(A)
    order = jnp.argsort(flat_e)
    e_sorted = flat_e[order]
    counts = jnp.zeros((E,), jnp.int32).at[flat_e].add(1)
    padded = (counts + BM - 1) // BM * BM
    pad_end = jnp.cumsum(padded)
    pad_start = pad_end - padded
    start = jnp.cumsum(counts) - counts
    dest = pad_start[e_sorted] + jnp.arange(A, dtype=jnp.int32) - start[e_sorted]
    NB = (A + E * (BM - 1)) // BM
    R = NB * BM
    row_tok = jnp.zeros((R,), jnp.int32).at[dest].set((order // MOE_TOPK).astype(jnp.int32))
    pos = jnp.zeros((A,), jnp.int32).at[order].set(dest)
    blk_start = jnp.arange(NB, dtype=jnp.int32) * BM
    last_used = jnp.maximum(pad_end[-1] // BM - 1, 0)
    blk_exp = jnp.minimum(jnp.searchsorted(pad_end, jnp.minimum(blk_start, last_used * BM), side='right'),
                          E - 1).astype(jnp.int32)
    blk_rows = jnp.where(blk_start < pad_end[-1],
                         jnp.clip(pad_start[blk_exp] + counts[blk_exp] - blk_start, 0, BM), 0).astype(jnp.int32)
    ys = _experts(h2, row_tok, blk_exp, blk_rows, w_gate, w_up, w_down)
    return ys, pos


def kernel(x_prompt, x_sample, state_ssm, state_conv, state_gdn, attn_norm, ffn_norm, final_norm,
           ssm_lambda_re, ssm_lambda_im, ssm_log_dt, ssm_b_re, ssm_b_im, ssm_c_re, ssm_c_im, ssm_d,
           ssm_w_glu, ssm_b_glu, gdn_w_in, gdn_conv_w, gdn_a_log, gdn_dt_bias, gdn_norm, gdn_w_out,
           moe_w_grp, moe_b_grp, moe_w_exp, moe_b_exp, moe_w_gate, moe_w_up, moe_w_down):
    bp, lp, D = x_prompt.shape
    bs, ls, _ = x_sample.shape
    n_p, n_s = bp * lp, bs * ls
    depth = attn_norm.shape[0]
    n_mix = 2
    H = gdn_a_log.shape[1]
    head_dim = gdn_norm.shape[1]
    val_dim = H * head_dim
    conv_dim = gdn_conv_w.shape[2]
    key_dim = (conv_dim - val_dim) // 2
    n_grp = moe_w_grp.shape[2]
    n_exp = moe_w_exp.shape[2]
    n_per = n_exp // n_grp
    assert head_dim == LANES and n_grp + n_exp <= LANES

    x = jnp.concatenate([x_prompt.reshape(n_p, D), x_sample.reshape(n_s, D)], 0).astype(F32)
    delta = None
    ssm_p, ssm_s, conv_p, conv_s, gdn_p, gdn_s = [], [], [], [], [], []
    for i in range(depth):
        j = i // n_mix
        if i % n_mix == 0:
            x, h = _add_norm(x, delta, attn_norm[i], F32)
            prm = _s5_params(ssm_lambda_re[j], ssm_lambda_im[j], ssm_log_dt[j], ssm_b_re[j], ssm_b_im[j],
                             ssm_c_re[j], ssm_c_im[j], ssm_d[j])
            z, hp_new, hs_new = _s5_mixer(h, (bp, lp), (bs, ls), state_ssm[j], prm)
            ssm_p.append(hp_new.astype(x.dtype))
            ssm_s.append(hs_new.astype(x.dtype))
            x = _matmul(z.astype(BF16), ssm_w_glu[j], D, mode="glu", res=x, z=z, bias=ssm_b_glu[j].astype(F32))
        else:
            x, h = _add_norm(x, delta, attn_norm[i], BF16)
            w_in = gdn_w_in[j]
            n_main = conv_dim + val_dim
            proj = _matmul(h, w_in, n_main)
            w_ba = jnp.concatenate([w_in[:, n_main:], jnp.zeros((D, LANES - 2 * H), F32)], 1)
            ba = _matmul(h, w_ba, LANES, tn=LANES)
            outs = []
            width = gdn_conv_w.shape[1]
            for (r0, bt, L, buf, s0, cl, sl, hpg) in (
                    (0, bp, lp, jnp.zeros((bp,) + state_conv.shape[2:], F32),
                     jnp.zeros((bp,) + state_gdn.shape[2:], F32), conv_p, gdn_p, 16),
                    (n_p, bs, ls, state_conv[j], state_gdn[j], conv_s, gdn_s, 32)):
                qkvc = _conv_qkv(proj, r0, buf, gdn_conv_w[j], bt, L, conv_dim, key_dim, head_dim)
                o, s_new = _delta_rule(qkvc, proj, r0, ba[r0:r0 + bt * L], s0, gdn_a_log[j], gdn_dt_bias[j],
                                       gdn_norm[j], bt, L, H, key_dim, val_dim, 64, hpg)
                outs.append(o)
                last = (r0 + (jnp.arange(bt) * L)[:, None] + jnp.arange(L - (width - 1), L)[None, :]).reshape(-1)
                cl.append(jnp.take(proj, last, axis=0)[:, :conv_dim].reshape(bt, width - 1, conv_dim))
                sl.append(s_new)
            x = _matmul(jnp.concatenate(outs, 0), gdn_w_out[j], D, mode="res", res=x)
        wr = jnp.concatenate([moe_w_grp[i], moe_w_exp[i],
                              jnp.zeros((D, LANES - n_grp - n_exp), F32)], 1)
        br = jnp.concatenate([moe_b_grp[i], moe_b_exp[i], jnp.zeros((LANES - n_grp - n_exp,), F32)])
        h2, eidx, wts = _add_norm(x, None, ffn_norm[i], F32, router=(wr, br.reshape(1, LANES), n_grp, n_per))[1:]
        ys, pos = _moe(h2, eidx[:, :MOE_TOPK], moe_w_gate[i], moe_w_up[i], moe_w_down[i])
        delta = (ys, pos, wts)
    x, y = _add_norm(x, delta, final_norm, F32)
    return (y[:n_p].reshape(bp, lp, D), y[n_p:].reshape(bs, ls, D),
            jnp.stack(ssm_p), jnp.stack(conv_p), jnp.stack(gdn_p),
            jnp.stack(ssm_s), jnp.stack(conv_s), jnp.stack(gdn_s))
```

```python
import functools

import jax
import jax.numpy as jnp
from jax import lax
from jax.experimental import pallas as pl
from jax.experimental.pallas import tpu as pltpu

F32 = jnp.float32
BF16 = jnp.bfloat16
HI = lax.Precision.HIGHEST

LANES = 128
SUBLANES = 8
VMEM_LIMIT = 56 * 1024 * 1024
NORM_EPS = 1e-6
S5_STEPS = 8
MOE_ROWS = 512
MOE_TOPK = 2


def _cparams(sem):
    return pltpu.CompilerParams(dimension_semantics=sem, vmem_limit_bytes=VMEM_LIMIT)


def _pick(n, cands):
    for c in cands:
        if n % c == 0:
            return c
    return n


def _row_copy(src_hbm, src_row, dst_vmem, dst_row, sem):
    return pltpu.make_async_copy(src_hbm.at[pl.ds(src_row, 1), :], dst_vmem.at[pl.ds(dst_row, 1), :], sem)


def _norm_kernel(*refs, has_moe, has_router, n_grp, n_per):
    it = iter(refs)
    pos_ref = next(it) if has_moe else None
    x_ref = next(it)
    if has_moe:
        cw_ref, ys_hbm = next(it), next(it)
    g_ref = next(it)
    if has_router:
        wr_ref, br_ref = next(it), next(it)
    xo_ref = next(it) if has_moe else None
    h_ref = next(it)
    if has_router:
        idx_ref, wts_ref = next(it), next(it)
    if has_moe:
        yg, sem = next(it), next(it)

    x = x_ref[...]
    if has_moe:
        tm = x.shape[0]
        i = pl.program_id(0)
        slot = i % 2

        def start_gather(blk, s):
            def start(r, carry):
                for k in range(MOE_TOPK):
                    _row_copy(ys_hbm, pos_ref[(blk * tm + r) * MOE_TOPK + k], yg.at[s], k * tm + r,
                              sem.at[s]).start()
                return carry

            lax.fori_loop(0, tm, start, 0, unroll=4)

        @pl.when(i == 0)
        def _():
            start_gather(0, 0)

        @pl.when(i + 1 < pl.num_programs(0))
        def _():
            start_gather(i + 1, 1 - slot)

        def wait(r, carry):
            _row_copy(ys_hbm, 0, yg.at[slot], r, sem.at[slot]).wait()
            return carry

        lax.fori_loop(0, MOE_TOPK * tm, wait, 0, unroll=8)
        cw = cw_ref[...]
        moe = yg[slot, 0:tm, :] * cw[:, 0:1]
        for k in range(1, MOE_TOPK):
            moe = moe + yg[slot, k * tm:(k + 1) * tm, :] * cw[:, k:k + 1]
        x = x + moe
        xo_ref[...] = x
    h = x * lax.rsqrt(jnp.mean(x * x, axis=-1, keepdims=True) + NORM_EPS) * g_ref[...]
    h_ref[...] = h.astype(h_ref.dtype)
    if not has_router:
        return

    logits = jnp.dot(h.astype(BF16), wr_ref[...].astype(BF16), preferred_element_type=F32) + br_ref[...]
    lane = lax.broadcasted_iota(jnp.int32, logits.shape, 1).astype(F32)
    neg = jnp.float32(-jnp.inf)
    big = jnp.float32(2 ** 20)
    is_grp = lane < n_grp
    lg = jnp.where(is_grp, logits, neg)
    gmax = jnp.max(lg, axis=-1, keepdims=True)
    gsel = jnp.min(jnp.where(lg == gmax, lane, big), axis=-1, keepdims=True)
    pg = 1.0 / jnp.sum(jnp.where(is_grp, jnp.exp(lg - gmax), 0.0), axis=-1, keepdims=True)
    lo = n_grp + gsel * n_per
    in_sel = (lane >= lo) & (lane < lo + n_per)
    le = jnp.where(in_sel, logits, neg)
    m1 = jnp.max(le, axis=-1, keepdims=True)
    i1 = jnp.min(jnp.where(le == m1, lane, big), axis=-1, keepdims=True)
    le2 = jnp.where(lane == i1, neg, le)
    m2 = jnp.max(le2, axis=-1, keepdims=True)
    i2 = jnp.min(jnp.where(le2 == m2, lane, big), axis=-1, keepdims=True)
    e2 = jnp.exp(m2 - m1)
    w1 = pg / (1.0 + e2)
    w2 = pg * e2 / (1.0 + e2)
    idx_ref[...] = jnp.where(lane == 0, i1 - n_grp, jnp.where(lane == 1, i2 - n_grp, 0.0)).astype(jnp.int32)
    wts_ref[...] = jnp.where(lane == 0, w1, jnp.where(lane == 1, w2, 0.0))


def _add_norm(x, moe, g, h_dtype, router=None):
    T, D = x.shape
    has_moe = moe is not None
    tm = _pick(T, (128, 64, 32, 16, 8)) if has_moe else _pick(T, (256, 128, 64, 32, 16, 8))
    row = pl.BlockSpec((tm, D), lambda i, *_: (i, 0))
    small = pl.BlockSpec((tm, LANES), lambda i, *_: (i, 0))
    full = lambda a: pl.BlockSpec(a.shape, lambda i, *_: (0,) * a.ndim)
    ins, specs, prefetch, scratch = [x], [row], [], []
    if has_moe:
        ys, pos, cw = moe
        prefetch = [pos]
        ins += [cw, ys]
        specs += [small, pl.BlockSpec(memory_space=pl.ANY)]
        scratch = [pltpu.VMEM((2, MOE_TOPK * tm, D), F32), pltpu.SemaphoreType.DMA((2,))]
    g2 = g.reshape(1, D).astype(F32)
    ins.append(g2)
    specs.append(full(g2))
    outs, ospecs = [], []
    if has_moe:
        outs.append(jax.ShapeDtypeStruct((T, D), F32))
        ospecs.append(row)
    outs.append(jax.ShapeDtypeStruct((T, D), h_dtype))
    ospecs.append(row)
    n_grp = n_per = 0
    if router is not None:
        wr, br, n_grp, n_per = router
        ins += [wr, br]
        specs += [full(wr), full(br)]
        outs += [jax.ShapeDtypeStruct((T, LANES), jnp.int32), jax.ShapeDtypeStruct((T, LANES), F32)]
        ospecs += [small, small]
    grid_spec = pltpu.PrefetchScalarGridSpec(
        num_scalar_prefetch=len(prefetch), grid=(T // tm,), in_specs=specs, out_specs=ospecs,
        scratch_shapes=scratch)
    res = pl.pallas_call(
        functools.partial(_norm_kernel, has_moe=has_moe, has_router=router is not None,
                          n_grp=n_grp, n_per=n_per),
        grid_spec=grid_spec, out_shape=outs,
        compiler_params=_cparams(("arbitrary",) if has_moe else ("parallel",)))(*prefetch, *ins)
    res = list(res)
    x_new = res.pop(0) if has_moe else x
    return (x_new, *res)


def _mm_kernel(*refs, mode):
    if mode == "plain":
        a_ref, w_ref, o_ref, wbf = refs
    elif mode == "res":
        a_ref, w_ref, r_ref, o_ref, wbf = refs
    else:
        a_ref, w_ref, r_ref, z_ref, b_ref, o_ref, wbf = refs

    @pl.when(pl.program_id(1) == 0)
    def _():
        wbf[...] = w_ref[...].astype(BF16)

    acc = jnp.dot(a_ref[...], wbf[...], preferred_element_type=F32)
    if mode == "plain":
        o_ref[...] = acc.astype(o_ref.dtype)
    elif mode == "res":
        o_ref[...] = r_ref[...] + acc
    else:
        o_ref[...] = r_ref[...] + z_ref[...] * jax.nn.sigmoid(acc + b_ref[...])


def _matmul(a, w, n_cols, mode="plain", layer=None, tn=512, res=None, z=None, bias=None, out_dtype=F32):
    M, K = a.shape
    tn = min(tn, n_cols)
    assert n_cols % tn == 0
    tm = _pick(M, (1024, 512, 256, 128, 64, 32, 16))
    a_spec = pl.BlockSpec((tm, K), lambda j, i: (i, 0))
    if layer is None:
        w_spec = pl.BlockSpec((K, tn), lambda j, i: (0, j))
    else:
        w_spec = pl.BlockSpec((None, K, tn), lambda j, i: (layer, 0, j))
    o_spec = pl.BlockSpec((tm, tn), lambda j, i: (i, j))
    ins, specs = [a, w], [a_spec, w_spec]
    if mode in ("res", "glu"):
        ins.append(res)
        specs.append(o_spec)
    if mode == "glu":
        ins += [z, bias.reshape(1, -1)]
        specs += [o_spec, pl.BlockSpec((1, tn), lambda j, i: (0, j))]
    return pl.pallas_call(
        functools.partial(_mm_kernel, mode=mode),
        grid=(n_cols // tn, M // tm), in_specs=specs, out_specs=o_spec,
        out_shape=jax.ShapeDtypeStruct((M, n_cols), out_dtype),
        scratch_shapes=[pltpu.VMEM((K, tn), BF16)],
        compiler_params=_cparams(("parallel", "arbitrary")))(*ins)


def _s5_params(lam_re, lam_im, log_dt, b_re, b_im, c_re, c_im, d_skip):
    G, P = lam_re.shape
    SG = b_re.shape[-1]
    n = S5_STEPS
    dt = jnp.exp(log_dt.astype(F32))[:, None]
    a, b = lam_re.astype(F32) * dt, lam_im.astype(F32) * dt
    lb_re, lb_im = jnp.exp(a) * jnp.cos(b), jnp.exp(a) * jnp.sin(b)
    nr, ni = lb_re - 1.0, lb_im
    den = lam_re * lam_re + lam_im * lam_im
    cf_re = (nr * lam_re + ni * lam_im) / den
    cf_im = (ni * lam_re - nr * lam_im) / den
    bb_re = cf_re[..., None] * b_re - cf_im[..., None] * b_im
    bb_im = cf_re[..., None] * b_im + cf_im[..., None] * b_re
    ks = jnp.arange(n + 1, dtype=F32)[:, None, None]
    pw_re = jnp.exp(ks * a) * jnp.cos(ks * b)
    pw_im = jnp.exp(ks * a) * jnp.sin(ks * b)
    pr, pi = pw_re[n - 1::-1][:n], pw_im[n - 1::-1][:n]
    t_re = pr[..., None] * bb_re[None] - pi[..., None] * bb_im[None]
    t_im = pr[..., None] * bb_im[None] + pi[..., None] * bb_re[None]
    B8 = jnp.concatenate([jnp.transpose(t_re, (1, 0, 3, 2)), jnp.transpose(t_im, (1, 0, 3, 2))], -1)
    B8 = B8.reshape(G, n * SG, 2 * P)
    cr, ci = c_re.astype(F32), c_im.astype(F32)
    qr, qi = pw_re[1:], pw_im[1:]
    m_re = cr[None] * qr[:, :, None, :] - ci[None] * qi[:, :, None, :]
    m_im = cr[None] * qi[:, :, None, :] + ci[None] * qr[:, :, None, :]
    top = jnp.concatenate([jnp.transpose(m_re, (1, 3, 0, 2)), -jnp.transpose(m_im, (1, 3, 0, 2))], 1)
    top = top.reshape(G, 2 * P, n * SG)
    w_re = pw_re[:n, :, :, None] * bb_re[None] - pw_im[:n, :, :, None] * bb_im[None]
    w_im = pw_re[:n, :, :, None] * bb_im[None] + pw_im[:n, :, :, None] * bb_re[None]
    ktap = (jnp.einsum('gcp,tgpd->tgcd', cr, w_re, precision=HI)
            - jnp.einsum('gcp,tgpd->tgcd', ci, w_im, precision=HI))
    rows = []
    for s in range(n):
        cols = []
        for i in range(n):
            cols.append(jnp.transpose(ktap[i - s], (0, 2, 1)) if s <= i else jnp.zeros((G, SG, SG), F32))
        rows.append(jnp.concatenate(cols, -1))
    toep = jnp.concatenate(rows, 1)
    CT = jnp.concatenate([top, toep], 1)
    z64 = jnp.zeros((G // 2, n * SG, P), F32)
    e, o = B8[0::2], B8[1::2]
    B8p = jnp.concatenate([
        jnp.concatenate([e[..., :P], z64, e[..., P:], z64], -1),
        jnp.concatenate([z64, o[..., :P], z64, o[..., P:]], -1)], 1)
    ce, co = CT[0::2], CT[1::2]
    zc = lambda r: jnp.zeros((G // 2, r, n * SG), F32)
    CTp = jnp.concatenate([
        jnp.concatenate([ce[:, :P], zc(P)], -1), jnp.concatenate([zc(P), co[:, :P]], -1),
        jnp.concatenate([ce[:, P:2 * P], zc(P)], -1), jnp.concatenate([zc(P), co[:, P:2 * P]], -1),
        jnp.concatenate([ce[:, 2 * P:], zc(n * SG)], -1), jnp.concatenate([zc(n * SG), co[:, 2 * P:]], -1)], 1)
    l8r, l8i = pw_re[n].reshape(1, G * P), pw_im[n].reshape(1, G * P)
    return B8p, CTp, l8r, l8i, d_skip.astype(F32).reshape(1, G * SG)


def _s5_kernel(x_ref, d_ref, h0r_ref, h0i_ref, b8_ref, ct_ref, lr_ref, li_ref,
               z_ref, hpr_ref, hpi_ref, hsr_ref, hsi_ref, u_scr, v_scr, h_scr, y_scr, *, bp, nk, bs):
    W = LANES
    n = S5_STEPS
    SG = W // n
    rp = bp * nk
    ra = rp + bs
    rc_rows = _pick(ra, (64, 32, 16, 8))
    chunk = lax.broadcasted_iota(jnp.int32, (rc_rows, W), 1) // SG

    def chunk_transpose(vs):
        vs = list(vs)
        d = 1
        while d < n:
            hi_bit = (chunk // d) % 2 == 1
            for i in range(n):
                if (i // d) % 2 == 0:
                    a, b = vs[i], vs[i + d]
                    vs[i] = jnp.where(hi_bit, pltpu.roll(b, d * SG, 1), a)
                    vs[i + d] = jnp.where(hi_bit, b, pltpu.roll(a, W - d * SG, 1))
            d *= 2
        return vs

    def relayout_in(rc, c):
        r0 = pl.multiple_of(rc * rc_rows, rc_rows)
        us = chunk_transpose([x_ref[pl.ds(r0 * n + i, rc_rows, stride=n), :] for i in range(n)])
        for g in range(n):
            u_scr[g, pl.ds(r0, rc_rows), :] = us[g]
        return c

    lax.fori_loop(0, ra // rc_rows, relayout_in, 0)

    npair = n // 2
    for p in range(npair):
        ub = jnp.concatenate([u_scr[2 * p], u_scr[2 * p + 1]], 1).astype(BF16)
        v = jnp.dot(ub, b8_ref[p].astype(BF16), preferred_element_type=F32)
        v_scr[2 * p] = v[:rp, :W]
        v_scr[2 * p + 1] = v[:rp, W:]
        sl = slice(p * W, (p + 1) * W)
        h0r, h0i = h0r_ref[:, sl], h0i_ref[:, sl]
        lr, li = lr_ref[:, sl], li_ref[:, sl]
        h_scr[2 * p, rp:, :] = h0r
        h_scr[2 * p + 1, rp:, :] = h0i
        hsr_ref[:, sl] = lr * h0r - li * h0i + v[rp:, :W]
        hsi_ref[:, sl] = lr * h0i + li * h0r + v[rp:, W:]

    lrs = [lr_ref[:, p * W:(p + 1) * W] for p in range(npair)]
    lis = [li_ref[:, p * W:(p + 1) * W] for p in range(npair)]

    def step(k, carry):
        out = []
        for p in range(npair):
            hr, hi = carry[2 * p], carry[2 * p + 1]
            h_scr[2 * p, pl.ds(k, bp, stride=nk), :] = hr
            h_scr[2 * p + 1, pl.ds(k, bp, stride=nk), :] = hi
            vr = v_scr[2 * p, pl.ds(k, bp, stride=nk), :]
            vi = v_scr[2 * p + 1, pl.ds(k, bp, stride=nk), :]
            out.append(lrs[p] * hr - lis[p] * hi + vr)
            out.append(lrs[p] * hi + lis[p] * hr + vi)
        return tuple(out)

    fin = lax.fori_loop(0, nk, step, tuple(jnp.zeros((bp, W), F32) for _ in range(n)), unroll=4)
    for p in range(npair):
        hpr_ref[:, p * W:(p + 1) * W] = fin[2 * p]
        hpi_ref[:, p * W:(p + 1) * W] = fin[2 * p + 1]

    for p in range(npair):
        lhs = jnp.concatenate([h_scr[2 * p], h_scr[2 * p + 1], u_scr[2 * p], u_scr[2 * p + 1]], 1)
        y = jnp.dot(lhs.astype(BF16), ct_ref[p].astype(BF16), preferred_element_type=F32)
        y_scr[2 * p] = y[:, :W]
        y_scr[2 * p + 1] = y[:, W:]

    dv = d_ref[...]

    def relayout_out(rc, c):
        r0 = pl.multiple_of(rc * rc_rows, rc_rows)
        zs = chunk_transpose([y_scr[g, pl.ds(r0, rc_rows), :] for g in range(n)])
        for i in range(n):
            a = x_ref[pl.ds(r0 * n + i, rc_rows, stride=n), :]
            z_ref[pl.ds(r0 * n + i, rc_rows, stride=n), :] = jax.nn.gelu(zs[i] + dv * a)
        return c

    lax.fori_loop(0, ra // rc_rows, relayout_out, 0)


def _s5_mixer(h, shape_p, shape_s, state_s, prm):
    B8p, CTp, l8r, l8i, dvec = prm
    T, D = h.shape
    G = 2 * B8p.shape[0]
    P = LANES // 2
    bp, lp = shape_p
    bs, ls = shape_s
    nk = lp // S5_STEPS
    ra = T // S5_STEPS
    assert ls == S5_STEPS and lp % S5_STEPS == 0 and ra % SUBLANES == 0 and bp * nk + bs == ra
    h0r = state_s[..., 0].astype(F32).reshape(bs, G * P)
    h0i = state_s[..., 1].astype(F32).reshape(bs, G * P)
    gpb = S5_STEPS
    sw = gpb * P
    xcol = pl.BlockSpec((T, LANES), lambda j: (0, j))
    st_s = pl.BlockSpec((bs, sw), lambda j: (0, j))
    st_p = pl.BlockSpec((bp, sw), lambda j: (0, j))
    vec = pl.BlockSpec((1, sw), lambda j: (0, j))
    z, hpr, hpi, hsr, hsi = pl.pallas_call(
        functools.partial(_s5_kernel, bp=bp, nk=nk, bs=bs),
        grid=(G // gpb,),
        in_specs=[xcol, pl.BlockSpec((1, LANES), lambda j: (0, j)), st_s, st_s,
                  pl.BlockSpec((gpb // 2, 2 * LANES, 2 * LANES), lambda j: (j, 0, 0)),
                  pl.BlockSpec((gpb // 2, 4 * LANES, 2 * LANES), lambda j: (j, 0, 0)),
                  vec, vec],
        out_specs=[xcol, st_p, st_p, st_s, st_s],
        out_shape=[jax.ShapeDtypeStruct((T, D), F32),
                   jax.ShapeDtypeStruct((bp, G * P), F32), jax.ShapeDtypeStruct((bp, G * P), F32),
                   jax.ShapeDtypeStruct((bs, G * P), F32), jax.ShapeDtypeStruct((bs, G * P), F32)],
        scratch_shapes=[pltpu.VMEM((gpb, ra, LANES), F32), pltpu.VMEM((gpb, bp * nk, LANES), F32),
                        pltpu.VMEM((gpb, ra, LANES), F32), pltpu.VMEM((gpb, ra, LANES), F32)],
        compiler_params=_cparams(("parallel",)))(h, dvec, h0r, h0i, B8p, CTp, l8r, l8i)
    st = lambda r, i, nb: jnp.stack([r.reshape(nb, G, P), i.reshape(nb, G, P)], -1)
    return z, st(hpr, hpi, bp), st(hsr, hsi, bs)


def _conv_kernel(x_ref, halo_ref, buf_ref, cw_ref, o_ref, *, seq_rows, n_norm_q, n_norm_k, q_scale, first_axis):
    x = x_ref[...]
    R, CB = x.shape
    width = cw_ref.shape[0]
    row = lax.broadcasted_iota(jnp.int32, (R, CB), 0)
    if seq_rows == SUBLANES:
        prev = buf_ref[...]
        trow = row % SUBLANES
    else:
        prev = jnp.where(pl.program_id(first_axis) == 0, buf_ref[...], halo_ref[...])
        trow = row
    acc = x * cw_ref[width - 1:width, :]
    for j in range(1, width):
        s = pltpu.roll(x, j, 0)
        if seq_rows == SUBLANES:
            p = pltpu.roll(prev, (R - SUBLANES + j) % R, 0)
            s = jnp.where(trow < j, p, s)
        else:
            p = pltpu.roll(prev, j, 0)
            row8 = lax.broadcasted_iota(jnp.int32, (SUBLANES, CB), 0)
            top = jnp.where(row8 < j, p, s[:SUBLANES])
            s = jnp.concatenate([top, s[SUBLANES:]], 0) if R > SUBLANES else top
        acc = acc + s * cw_ref[width - 1 - j:width - j, :]
    c = acc * jax.nn.sigmoid(acc)
    cb = pl.program_id(2 if seq_rows != SUBLANES else 1)
    segs = []
    for hh in range(CB // LANES):
        seg = c[:, hh * LANES:(hh + 1) * LANES]
        segs.append(seg * lax.rsqrt(jnp.sum(seg * seg, axis=-1, keepdims=True) + 1e-6))
    normed = jnp.concatenate(segs, -1) if len(segs) > 1 else segs[0]
    scale = jnp.where(cb < n_norm_q, jnp.float32(q_scale), jnp.float32(1.0))
    o_ref[...] = jnp.where(cb < n_norm_q + n_norm_k, normed * scale, c)


def _conv_qkv(proj, row0, conv_buf, conv_w, bt, L, conv_dim, key_dim, head_dim):
    width = conv_w.shape[0]
    assert width - 1 <= SUBLANES and L % SUBLANES == 0
    CB = _pick(key_dim, (512, 256, 128))
    ncb = conv_dim // CB
    nq = nk = key_dim // CB
    buf8 = jnp.concatenate([jnp.zeros((bt, SUBLANES - (width - 1), conv_dim), F32), conv_buf.astype(F32)], 1)
    buf8 = buf8.reshape(bt * SUBLANES, conv_dim)
    cw = conv_w.astype(F32)
    q_scale = float(head_dim) ** -0.5
    if L == SUBLANES:
        R = _pick(bt * L, (128, 64, 32, 16, 8))
        assert row0 % R == 0
        blk = pl.BlockSpec((R, CB), lambda i, c: (i, c))
        xin = pl.BlockSpec((R, CB), lambda i, c: (i + row0 // R, c))
        return pl.pallas_call(
            functools.partial(_conv_kernel, seq_rows=SUBLANES, n_norm_q=nq, n_norm_k=nk, q_scale=q_scale,
                              first_axis=0),
            grid=(bt * L // R, ncb),
            in_specs=[xin, blk, blk, pl.BlockSpec((width, CB), lambda i, c: (0, c))],
            out_specs=blk, out_shape=jax.ShapeDtypeStruct((bt * L, conv_dim), F32),
            compiler_params=_cparams(("parallel", "parallel")))(proj, buf8, buf8, cw)
    R = _pick(L, (256, 128, 64, 32, 16, 8))
    nt = L // R
    assert row0 % R == 0
    rb0 = row0 // R
    blk = pl.BlockSpec((R, CB), lambda b, t, c: (b * nt + t, c))
    xin = pl.BlockSpec((R, CB), lambda b, t, c: (rb0 + b * nt + t, c))
    halo = pl.BlockSpec((SUBLANES, CB),
                        lambda b, t, c: (jnp.maximum((rb0 + b * nt + t) * (R // SUBLANES) - 1, 0), c))
    bufs = pl.BlockSpec((SUBLANES, CB), lambda b, t, c: (b, c))
    return pl.pallas_call(
        functools.partial(_conv_kernel, seq_rows=R, n_norm_q=nq, n_norm_k=nk, q_scale=q_scale, first_axis=1),
        grid=(bt, nt, ncb),
        in_specs=[xin, halo, bufs, pl.BlockSpec((width, CB), lambda b, t, c: (0, c))],
        out_specs=blk, out_shape=jax.ShapeDtypeStruct((bt * L, conv_dim), F32),
        compiler_params=_cparams(("parallel", "parallel", "parallel")))(proj, proj, buf8, cw)


def _dot_nt(a, b):
    return lax.dot_general(a.astype(BF16), b.astype(BF16), (((1,), (1,)), ((), ())), preferred_element_type=F32)


def _dot_tn(a, b):
    return lax.dot_general(a.astype(BF16), b.astype(BF16), (((0,), (0,)), ((), ())), preferred_element_type=F32)


def _dot(a, b):
    return jnp.dot(a.astype(BF16), b.astype(BF16), preferred_element_type=F32)


def _delta_kernel(q_ref, k_ref, v_ref, z_ref, col_ref, rowp_ref, s0_ref, ng_ref, o_ref, so_ref, s_scr,
                  *, hpg, C):
    n = pl.program_id(2)

    @pl.when(n == 0)
    def _():
        s_scr[...] = s0_ref[0]

    ri = lax.broadcasted_iota(jnp.int32, (C, C), 0)
    ci = lax.broadcasted_iota(jnp.int32, (C, C), 1)
    incl = ri >= ci
    strict = ri > ci
    eye = (ri == ci).astype(F32)
    col = col_ref[0, 0]
    rowp = rowp_ref[0, 0, 0]
    ng = ng_ref[...]
    rep = hpg // (q_ref.shape[-1] // LANES)
    hs = range(hpg)
    lanes = lambda i: slice(i * LANES, (i + 1) * LANES)
    qs = [q_ref[:, lanes(i)] for i in range(hpg // rep)]
    ks = [k_ref[:, lanes(i)] for i in range(hpg // rep)]
    gc = [col[:, h:h + 1] for h in hs]
    beta = [col[:, hpg + h:hpg + h + 1] for h in hs]
    eg = [jnp.exp(gc[h]) for h in hs]
    decay = [jnp.where(incl, jnp.exp(jnp.where(incl, gc[h] - rowp[h:h + 1, :], 0.0)), 0.0) for h in hs]
    kb = [ks[h // rep] * beta[h] for h in hs]
    m = [jnp.where(strict, _dot_nt(kb[h], ks[h // rep]) * decay[h], 0.0) for h in hs]
    qk = [_dot_nt(qs[i], ks[i]) for i in range(hpg // rep)]
    blk2 = (ri // 2) == (ci // 2)
    tinv = [eye - jnp.where(blk2, m[h], 0.0) for h in hs]
    s = 2
    while s < C:
        off = ((ri // (2 * s)) == (ci // (2 * s))) & ((ri // s) % 2 == 1) & ((ci // s) % 2 == 0)
        tb = [_dot(tinv[h], jnp.where(off, m[h], 0.0)) for h in hs]
        tinv = [tinv[h] - _dot(tb[h], tinv[h]) for h in hs]
        s *= 2
    u = [_dot(tinv[h], v_ref[:, lanes(h)] * beta[h]) for h in hs]
    w = [_dot(tinv[h], kb[h] * eg[h]) for h in hs]
    S = [s_scr[h] for h in hs]
    ws = [_dot(jnp.concatenate([w[h], qs[h // rep] * eg[h]], 0), S[h]) for h in hs]
    v_new = [u[h] - ws[h][:C] for h in hs]
    o = [ws[h][C:] + _dot(jnp.where(incl, qk[h // rep] * decay[h], 0.0), v_new[h]) for h in hs]
    g_last = [gc[h][C - 1:C, :] for h in hs]
    s_new = [S[h] * jnp.exp(g_last[h]) + _dot_tn(ks[h // rep] * jnp.exp(g_last[h] - gc[h]), v_new[h]) for h in hs]
    for h in hs:
        s_scr[h] = s_new[h]
        zh = z_ref[:, lanes(h)]
        on = o[h] * lax.rsqrt(jnp.mean(o[h] * o[h], axis=-1, keepdims=True) + NORM_EPS) * ng
        o_ref[:, lanes(h)] = (on * (zh * jax.nn.sigmoid(zh))).astype(o_ref.dtype)

    @pl.when(n == pl.num_programs(2) - 1)
    def _():
        so_ref[0] = s_scr[...]


def _delta_rule(qkvc, proj, row0, ba, s0, s_layer, a_log, dt_bias, norm_g, bt, L, n_heads, key_dim, val_dim,
                chunk, hpg):
    H = n_heads
    C = chunk if L % chunk == 0 else L
    N = L // C
    hpg = min(hpg, H)
    HG = H // hpg
    assert row0 % C == 0 and hpg % SUBLANES == 0
    conv_dim = 2 * key_dim + val_dim
    rep = H // (key_dim // LANES)
    qw = (hpg // rep) * LANES
    vw = hpg * LANES
    beta = jax.nn.sigmoid(ba[:, :H])
    g = -jnp.exp(a_log.astype(F32)) * jax.nn.softplus(ba[:, H:2 * H] + dt_bias.astype(F32))
    gcum = jnp.cumsum(g.reshape(bt, N, C, H), axis=2)
    col = jnp.concatenate([gcum.reshape(bt, L, HG, hpg), beta.reshape(bt, L, HG, hpg)], -1)
    col = col.transpose(0, 2, 1, 3)
    rowp = gcum.reshape(bt, N, C, HG, hpg).transpose(0, 3, 1, 4, 2)
    kq0 = key_dim // qw
    v0 = 2 * key_dim // vw
    z0 = conv_dim // vw
    rb0 = row0 // C
    o, s_out = pl.pallas_call(
        functools.partial(_delta_kernel, hpg=hpg, C=C),
        grid=(bt, HG, N),
        in_specs=[pl.BlockSpec((C, qw), lambda b, h, n: (b * N + n, h)),
                  pl.BlockSpec((C, qw), lambda b, h, n: (b * N + n, kq0 + h)),
                  pl.BlockSpec((C, vw), lambda b, h, n: (b * N + n, v0 + h)),
                  pl.BlockSpec((C, vw), lambda b, h, n: (rb0 + b * N + n, z0 + h)),
                  pl.BlockSpec((1, 1, C, 2 * hpg), lambda b, h, n: (b, h, n, 0)),
                  pl.BlockSpec((1, 1, 1, hpg, C), lambda b, h, n: (b, h, n, 0, 0)),
                  pl.BlockSpec((None, 1, hpg, LANES, LANES), lambda b, h, n: (s_layer, b, h, 0, 0)),
                  pl.BlockSpec((1, LANES), lambda b, h, n: (0, 0))],
        out_specs=[pl.BlockSpec((C, vw), lambda b, h, n: (b * N + n, h)),
                   pl.BlockSpec((1, hpg, LANES, LANES), lambda b, h, n: (b, h, 0, 0))],
        out_shape=[jax.ShapeDtypeStruct((bt * L, val_dim), BF16 if C % 16 == 0 else F32),
                   jax.ShapeDtypeStruct((bt, H, LANES, LANES), F32)],
        scratch_shapes=[pltpu.VMEM((hpg, LANES, LANES), F32)],
        compiler_params=_cparams(("parallel", "parallel", "arbitrary")))(
            qkvc, qkvc, qkvc, proj, col, rowp, s0.astype(F32), norm_g.astype(F32).reshape(1, LANES))
    return o.astype(BF16), s_out


def _expert_kernel(be_ref, bn_ref, tok_ref, x_hbm, wg_ref, wu_ref, wd_ref, o_ref, xg, acc_g, acc_u, hbuf, sem,
                   *, nkc):
    i = pl.program_id(0)
    c = pl.program_id(1)
    rows = bn_ref[i]
    valid = rows > 0
    bm = xg.shape[0]
    kc = wg_ref.shape[1]

    def start_gather(blk):
        def start(r, carry):
            _row_copy(x_hbm, tok_ref[blk * bm + r], xg, r, sem).start()
            return carry

        lax.fori_loop(0, bn_ref[blk], start, 0)

    @pl.when((i == 0) & (c == 0))
    def _():
        xg[...] = jnp.zeros_like(xg)
        start_gather(0)

    @pl.when((c == nkc) & (i + 1 < pl.num_programs(0)))
    def _():
        start_gather(i + 1)

    @pl.when(valid & (c == 0))
    def _():
        def wait(r, carry):
            _row_copy(x_hbm, 0, xg, r, sem).wait()
            return carry

        lax.fori_loop(0, rows, wait, 0)

    for cc in range(nkc):
        @pl.when(valid & (c == cc))
        def _(cc=cc):
            xk = xg[:, cc * kc:(cc + 1) * kc].astype(BF16)
            pg = jnp.dot(xk, wg_ref[0].astype(BF16), preferred_element_type=F32)
            pu = jnp.dot(xk, wu_ref[0].astype(BF16), preferred_element_type=F32)
            if cc == 0:
                acc_g[...] = pg
                acc_u[...] = pu
            else:
                acc_g[...] += pg
                acc_u[...] += pu
            if cc == nkc - 1:
                gte = acc_g[...]
                hbuf[...] = (gte * jax.nn.sigmoid(gte) * acc_u[...]).astype(BF16)

    @pl.when(valid & (c >= nkc))
    def _():
        o_ref[...] = jnp.dot(hbuf[...], wd_ref[0].astype(BF16), preferred_element_type=F32)

    @pl.when(jnp.logical_not(valid) & (c >= nkc))
    def _():
        o_ref[...] = jnp.zeros_like(o_ref)


def _experts(x, row_tok, blk_exp, blk_rows, w_gate, w_up, w_down, layer):
    T, D = x.shape
    R = row_tok.shape[0]
    _, E, _, FF = w_gate.shape
    NB = R // MOE_ROWS
    kc = _pick(D, (1024, 512, 256, 128))
    nkc = D // kc
    nc = min(2 * kc, D)
    nnc = D // nc

    def kidx(c, v):
        return jnp.where(v > 0, jnp.minimum(c, nkc - 1), nkc - 1)

    def nidx(c, v):
        return jnp.where(v > 0, jnp.maximum(c - nkc, 0), nnc - 1)

    grid_spec = pltpu.PrefetchScalarGridSpec(
        num_scalar_prefetch=3, grid=(NB, nkc + nnc),
        in_specs=[pl.BlockSpec(memory_space=pl.ANY),
                  pl.BlockSpec((None, 1, kc, FF), lambda i, c, be, bn, tk: (layer, be[i], kidx(c, bn[i]), 0)),
                  pl.BlockSpec((None, 1, kc, FF), lambda i, c, be, bn, tk: (layer, be[i], kidx(c, bn[i]), 0)),
                  pl.BlockSpec((None, 1, FF, nc), lambda i, c, be, bn, tk: (layer, be[i], 0, nidx(c, bn[i])))],
        out_specs=pl.BlockSpec((MOE_ROWS, nc), lambda i, c, be, bn, tk: (i, jnp.maximum(c - nkc, 0))),
        scratch_shapes=[pltpu.VMEM((MOE_ROWS, D), F32),
                        pltpu.VMEM((MOE_ROWS, FF), F32), pltpu.VMEM((MOE_ROWS, FF), F32),
                        pltpu.VMEM((MOE_ROWS, FF), BF16), pltpu.SemaphoreType.DMA])
    return pl.pallas_call(
        functools.partial(_expert_kernel, nkc=nkc), grid_spec=grid_spec,
        out_shape=jax.ShapeDtypeStruct((R, D), F32),
        compiler_params=_cparams(("arbitrary", "arbitrary")))(blk_exp, blk_rows, row_tok, x, w_gate, w_up, w_down)


def _moe(h2, eidx, w_gate, w_up, w_down, layer):
    T, D = h2.shape
    E = w_gate.shape[1]
    A = T * MOE_TOPK
    BM = MOE_ROWS
    flat_e = eidx.reshape(A)
    order = jnp.argsort(flat_e)
    e_sorted = flat_e[order]
    counts = jnp.zeros((E,), jnp.int32).at[flat_e].add(1)
    padded = (counts + BM - 1) // BM * BM
    pad_end = jnp.cumsum(padded)
    pad_start = pad_end - padded
    start = jnp.cumsum(counts) - counts
    dest = pad_start[e_sorted] + jnp.arange(A, dtype=jnp.int32) - start[e_sorted]
    NB = (A + E * (BM - 1)) // BM
    R = NB * BM
    row_tok = jnp.zeros((R,), jnp.int32).at[dest].set((order // MOE_TOPK).astype(jnp.int32))
    pos = jnp.zeros((A,), jnp.int32).at[order].set(dest)
    blk_start = jnp.arange(NB, dtype=jnp.int32) * BM
    last_used = jnp.maximum(pad_end[-1] // BM - 1, 0)
    blk_exp = jnp.minimum(jnp.searchsorted(pad_end, jnp.minimum(blk_start, last_used * BM), side='right'),
                          E - 1).astype(jnp.int32)
    blk_rows = jnp.where(blk_start < pad_end[-1],
                         jnp.clip(pad_start[blk_exp] + counts[blk_exp] - blk_start, 0, BM), 0).astype(jnp.int32)
    ys = _experts(h2, row_tok, blk_exp, blk_rows, w_gate, w_up, w_down, layer)
    return ys, pos


def kernel(x_prompt, x_sample, state_ssm, state_conv, state_gdn, attn_norm, ffn_norm, final_norm,
           ssm_lambda_re, ssm_lambda_im, ssm_log_dt, ssm_b_re, ssm_b_im, ssm_c_re, ssm_c_im, ssm_d,
           ssm_w_glu, ssm_b_glu, gdn_w_in, gdn_conv_w, gdn_a_log, gdn_dt_bias, gdn_norm, gdn_w_out,
           moe_w_grp, moe_b_grp, moe_w_exp, moe_b_exp, moe_w_gate, moe_w_up, moe_w_down):
    bp, lp, D = x_prompt.shape
    bs, ls, _ = x_sample.shape
    n_p, n_s = bp * lp, bs * ls
    depth = attn_norm.shape[0]
    n_mix = 2
    H = gdn_a_log.shape[1]
    head_dim = gdn_norm.shape[1]
    val_dim = H * head_dim
    conv_dim = gdn_conv_w.shape[2]
    key_dim = (conv_dim - val_dim) // 2
    n_grp = moe_w_grp.shape[2]
    n_exp = moe_w_exp.shape[2]
    n_per = n_exp // n_grp
    assert head_dim == LANES and n_grp + n_exp <= LANES

    x = jnp.concatenate([x_prompt.reshape(n_p, D), x_sample.reshape(n_s, D)], 0).astype(F32)
    delta = None
    ssm_p, ssm_s, conv_p, conv_s, gdn_p, gdn_s = [], [], [], [], [], []
    for i in range(depth):
        j = i // n_mix
        if i % n_mix == 0:
            x, h = _add_norm(x, delta, attn_norm[i], F32)
            prm = _s5_params(ssm_lambda_re[j], ssm_lambda_im[j], ssm_log_dt[j], ssm_b_re[j], ssm_b_im[j],
                             ssm_c_re[j], ssm_c_im[j], ssm_d[j])
            z, hp_new, hs_new = _s5_mixer(h, (bp, lp), (bs, ls), state_ssm[j], prm)
            ssm_p.append(hp_new.astype(x.dtype))
            ssm_s.append(hs_new.astype(x.dtype))
            x = _matmul(z.astype(BF16), ssm_w_glu, D, mode="glu", layer=j, res=x, z=z,
                        bias=ssm_b_glu[j].astype(F32))
        else:
            x, h = _add_norm(x, delta, attn_norm[i], BF16)
            n_main = conv_dim + val_dim
            proj = _matmul(h, gdn_w_in, n_main, layer=j)
            w_ba = jnp.concatenate([gdn_w_in[j, :, n_main:], jnp.zeros((D, LANES - 2 * H), F32)], 1)
            ba = _matmul(h, w_ba, LANES, tn=LANES)
            outs = []
            width = gdn_conv_w.shape[1]
            for (r0, bt, L, buf, s0, s_layer, cl, sl, hpg) in (
                    (0, bp, lp, jnp.zeros((bp,) + state_conv.shape[2:], F32),
                     jnp.zeros((1, bp) + state_gdn.shape[2:], F32), 0, conv_p, gdn_p, 16),
                    (n_p, bs, ls, state_conv[j], state_gdn, j, conv_s, gdn_s, 32)):
                qkvc = _conv_qkv(proj, r0, buf, gdn_conv_w[j], bt, L, conv_dim, key_dim, head_dim)
                o, s_new = _delta_rule(qkvc, proj, r0, ba[r0:r0 + bt * L], s0, s_layer, gdn_a_log[j],
                                       gdn_dt_bias[j], gdn_norm[j], bt, L, H, key_dim, val_dim, 64, hpg)
                outs.append(o)
                last = (r0 + (jnp.arange(bt) * L)[:, None] + jnp.arange(L - (width - 1), L)[None, :]).reshape(-1)
                cl.append(jnp.take(proj, last, axis=0)[:, :conv_dim].reshape(bt, width - 1, conv_dim))
                sl.append(s_new)
            x = _matmul(jnp.concatenate(outs, 0), gdn_w_out, D, mode="res", layer=j, res=x)
        wr = jnp.concatenate([moe_w_grp[i], moe_w_exp[i],
                              jnp.zeros((D, LANES - n_grp - n_exp), F32)], 1)
        br = jnp.concatenate([moe_b_grp[i], moe_b_exp[i], jnp.zeros((LANES - n_grp - n_exp,), F32)])
        h2, eidx, wts = _add_norm(x, None, ffn_norm[i], F32, router=(wr, br.reshape(1, LANES), n_grp, n_per))[1:]
        ys, pos = _moe(h2, eidx[:, :MOE_TOPK], moe_w_gate, moe_w_up, moe_w_down, i)
        delta = (ys, pos, wts)
    x, y = _add_norm(x, delta, final_norm, F32)
    return (y[:n_p].reshape(bp, lp, D), y[n_p:].reshape(bs, ls, D),
            jnp.stack(ssm_p), jnp.stack(conv_p), jnp.stack(gdn_p),
            jnp.stack(ssm_s), jnp.stack(conv_s), jnp.stack(gdn_s))
```

```python
import functools

import jax
import jax.numpy as jnp
from jax import lax
from jax.experimental import pallas as pl
from jax.experimental.pallas import tpu as pltpu

F32 = jnp.float32
BF16 = jnp.bfloat16
HI = lax.Precision.HIGHEST

LANES = 128
SUBLANES = 8
VMEM_LIMIT = 56 * 1024 * 1024
NORM_EPS = 1e-6
S5_STEPS = 8
MOE_ROWS = 512
MOE_TOPK = 2


def _cparams(sem):
    return pltpu.CompilerParams(dimension_semantics=sem, vmem_limit_bytes=VMEM_LIMIT)


def _pick(n, cands):
    for c in cands:
        if n % c == 0:
            return c
    return n


def _row_copy(src_hbm, src_row, dst_vmem, dst_row, sem):
    return pltpu.make_async_copy(src_hbm.at[pl.ds(src_row, 1), :], dst_vmem.at[pl.ds(dst_row, 1), :], sem)


def _norm_kernel(*refs, has_moe, has_router, n_grp, n_per):
    it = iter(refs)
    pos_ref = next(it) if has_moe else None
    x_ref = next(it)
    if has_moe:
        cw_ref, ys_hbm = next(it), next(it)
    g_ref = next(it)
    if has_router:
        wr_ref, br_ref = next(it), next(it)
    xo_ref = next(it) if has_moe else None
    h_ref = next(it)
    if has_router:
        idx_ref, wts_ref = next(it), next(it)
    if has_moe:
        yg, sem = next(it), next(it)

    x = x_ref[...]
    if has_moe:
        tm = x.shape[0]
        i = pl.program_id(0)
        slot = i % 2

        def start_gather(blk, s):
            def start(r, carry):
                for k in range(MOE_TOPK):
                    _row_copy(ys_hbm, pos_ref[(blk * tm + r) * MOE_TOPK + k], yg.at[s], k * tm + r,
                              sem.at[s]).start()
                return carry

            lax.fori_loop(0, tm, start, 0, unroll=4)

        @pl.when(i == 0)
        def _():
            start_gather(0, 0)

        @pl.when(i + 1 < pl.num_programs(0))
        def _():
            start_gather(i + 1, 1 - slot)

        def wait(r, carry):
            _row_copy(ys_hbm, 0, yg.at[slot], r, sem.at[slot]).wait()
            return carry

        lax.fori_loop(0, MOE_TOPK * tm, wait, 0, unroll=8)
        cw = cw_ref[...]
        moe = yg[slot, 0:tm, :] * cw[:, 0:1]
        for k in range(1, MOE_TOPK):
            moe = moe + yg[slot, k * tm:(k + 1) * tm, :] * cw[:, k:k + 1]
        x = x + moe
        xo_ref[...] = x
    h = x * lax.rsqrt(jnp.mean(x * x, axis=-1, keepdims=True) + NORM_EPS) * g_ref[...]
    if h_ref.dtype == jnp.uint32:
        tm, D = h.shape
        nt = D // 2 // LANES
        hb = lax.bitcast_convert_type(h.astype(BF16).astype(F32), jnp.uint32)
        for t in range(nt):
            lo = hb[:, t * LANES:(t + 1) * LANES] >> 16
            hi = hb[:, D // 2 + t * LANES:D // 2 + (t + 1) * LANES]
            h_ref[pl.ds(t, tm, stride=nt), :] = lo | hi
    else:
        h_ref[...] = h.astype(h_ref.dtype)
    if not has_router:
        return

    logits = jnp.dot(h.astype(BF16), wr_ref[...].astype(BF16), preferred_element_type=F32) + br_ref[...]
    lane = lax.broadcasted_iota(jnp.int32, logits.shape, 1).astype(F32)
    neg = jnp.float32(-jnp.inf)
    big = jnp.float32(2 ** 20)
    is_grp = lane < n_grp
    lg = jnp.where(is_grp, logits, neg)
    gmax = jnp.max(lg, axis=-1, keepdims=True)
    gsel = jnp.min(jnp.where(lg == gmax, lane, big), axis=-1, keepdims=True)
    pg = 1.0 / jnp.sum(jnp.where(is_grp, jnp.exp(lg - gmax), 0.0), axis=-1, keepdims=True)
    lo = n_grp + gsel * n_per
    in_sel = (lane >= lo) & (lane < lo + n_per)
    le = jnp.where(in_sel, logits, neg)
    m1 = jnp.max(le, axis=-1, keepdims=True)
    i1 = jnp.min(jnp.where(le == m1, lane, big), axis=-1, keepdims=True)
    le2 = jnp.where(lane == i1, neg, le)
    m2 = jnp.max(le2, axis=-1, keepdims=True)
    i2 = jnp.min(jnp.where(le2 == m2, lane, big), axis=-1, keepdims=True)
    e2 = jnp.exp(m2 - m1)
    w1 = pg / (1.0 + e2)
    w2 = pg * e2 / (1.0 + e2)
    idx_ref[...] = jnp.where(lane == 0, i1 - n_grp, jnp.where(lane == 1, i2 - n_grp, 0.0)).astype(jnp.int32)
    wts_ref[...] = jnp.where(lane == 0, w1, jnp.where(lane == 1, w2, 0.0))


def _add_norm(x, moe, g, h_dtype, router=None):
    T, D = x.shape
    has_moe = moe is not None
    tm = _pick(T, (128, 64, 32, 16, 8)) if has_moe else _pick(T, (256, 128, 64, 32, 16, 8))
    row = pl.BlockSpec((tm, D), lambda i, *_: (i, 0))
    small = pl.BlockSpec((tm, LANES), lambda i, *_: (i, 0))
    full = lambda a: pl.BlockSpec(a.shape, lambda i, *_: (0,) * a.ndim)
    ins, specs, prefetch, scratch = [x], [row], [], []
    if has_moe:
        ys, pos, cw = moe
        prefetch = [pos]
        ins += [cw, ys]
        specs += [small, pl.BlockSpec(memory_space=pl.ANY)]
        scratch = [pltpu.VMEM((2, MOE_TOPK * tm, D), F32), pltpu.SemaphoreType.DMA((2,))]
    g2 = g.reshape(1, D).astype(F32)
    ins.append(g2)
    specs.append(full(g2))
    outs, ospecs = [], []
    if has_moe:
        outs.append(jax.ShapeDtypeStruct((T, D), F32))
        ospecs.append(row)
    if h_dtype == jnp.uint32:
        nt = D // 2 // LANES
        outs.append(jax.ShapeDtypeStruct((T * nt, LANES), jnp.uint32))
        ospecs.append(pl.BlockSpec((tm * nt, LANES), lambda i, *_: (i, 0)))
    else:
        outs.append(jax.ShapeDtypeStruct((T, D), h_dtype))
        ospecs.append(row)
    n_grp = n_per = 0
    if router is not None:
        wr, br, n_grp, n_per = router
        ins += [wr, br]
        specs += [full(wr), full(br)]
        outs += [jax.ShapeDtypeStruct((T, LANES), jnp.int32), jax.ShapeDtypeStruct((T, LANES), F32)]
        ospecs += [small, small]
    grid_spec = pltpu.PrefetchScalarGridSpec(
        num_scalar_prefetch=len(prefetch), grid=(T // tm,), in_specs=specs, out_specs=ospecs,
        scratch_shapes=scratch)
    res = pl.pallas_call(
        functools.partial(_norm_kernel, has_moe=has_moe, has_router=router is not None,
                          n_grp=n_grp, n_per=n_per),
        grid_spec=grid_spec, out_shape=outs,
        compiler_params=_cparams(("arbitrary",) if has_moe else ("parallel",)))(*prefetch, *ins)
    res = list(res)
    x_new = res.pop(0) if has_moe else x
    return (x_new, *res)


def _mm_kernel(*refs, mode):
    if mode == "plain":
        a_ref, w_ref, o_ref, wbf = refs
    elif mode == "res":
        a_ref, w_ref, r_ref, o_ref, wbf = refs
    else:
        a_ref, w_ref, r_ref, z_ref, b_ref, o_ref, wbf = refs

    @pl.when(pl.program_id(1) == 0)
    def _():
        wbf[...] = w_ref[...].astype(BF16)

    acc = jnp.dot(a_ref[...], wbf[...], preferred_element_type=F32)
    if mode == "plain":
        o_ref[...] = acc.astype(o_ref.dtype)
    elif mode == "res":
        o_ref[...] = r_ref[...] + acc
    else:
        o_ref[...] = r_ref[...] + z_ref[...] * jax.nn.sigmoid(acc + b_ref[...])


def _matmul(a, w, n_cols, mode="plain", layer=None, tn=512, res=None, z=None, bias=None, out_dtype=F32):
    M, K = a.shape
    tn = min(tn, n_cols)
    assert n_cols % tn == 0
    tm = _pick(M, (1024, 512, 256, 128, 64, 32, 16))
    a_spec = pl.BlockSpec((tm, K), lambda j, i: (i, 0))
    if layer is None:
        w_spec = pl.BlockSpec((K, tn), lambda j, i: (0, j))
    else:
        w_spec = pl.BlockSpec((None, K, tn), lambda j, i: (layer, 0, j))
    o_spec = pl.BlockSpec((tm, tn), lambda j, i: (i, j))
    ins, specs = [a, w], [a_spec, w_spec]
    if mode in ("res", "glu"):
        ins.append(res)
        specs.append(o_spec)
    if mode == "glu":
        ins += [z, bias.reshape(1, -1)]
        specs += [o_spec, pl.BlockSpec((1, tn), lambda j, i: (0, j))]
    return pl.pallas_call(
        functools.partial(_mm_kernel, mode=mode),
        grid=(n_cols // tn, M // tm), in_specs=specs, out_specs=o_spec,
        out_shape=jax.ShapeDtypeStruct((M, n_cols), out_dtype),
        scratch_shapes=[pltpu.VMEM((K, tn), BF16)],
        compiler_params=_cparams(("parallel", "arbitrary")))(*ins)


def _s5_params(lam_re, lam_im, log_dt, b_re, b_im, c_re, c_im, d_skip):
    G, P = lam_re.shape
    SG = b_re.shape[-1]
    n = S5_STEPS
    dt = jnp.exp(log_dt.astype(F32))[:, None]
    a, b = lam_re.astype(F32) * dt, lam_im.astype(F32) * dt
    lb_re, lb_im = jnp.exp(a) * jnp.cos(b), jnp.exp(a) * jnp.sin(b)
    nr, ni = lb_re - 1.0, lb_im
    den = lam_re * lam_re + lam_im * lam_im
    cf_re = (nr * lam_re + ni * lam_im) / den
    cf_im = (ni * lam_re - nr * lam_im) / den
    bb_re = cf_re[..., None] * b_re - cf_im[..., None] * b_im
    bb_im = cf_re[..., None] * b_im + cf_im[..., None] * b_re
    ks = jnp.arange(n + 1, dtype=F32)[:, None, None]
    pw_re = jnp.exp(ks * a) * jnp.cos(ks * b)
    pw_im = jnp.exp(ks * a) * jnp.sin(ks * b)
    pr, pi = pw_re[n - 1::-1][:n], pw_im[n - 1::-1][:n]
    t_re = pr[..., None] * bb_re[None] - pi[..., None] * bb_im[None]
    t_im = pr[..., None] * bb_im[None] + pi[..., None] * bb_re[None]
    B8 = jnp.concatenate([jnp.transpose(t_re, (1, 0, 3, 2)), jnp.transpose(t_im, (1, 0, 3, 2))], -1)
    B8 = B8.reshape(G, n * SG, 2 * P)
    cr, ci = c_re.astype(F32), c_im.astype(F32)
    qr, qi = pw_re[1:], pw_im[1:]
    m_re = cr[None] * qr[:, :, None, :] - ci[None] * qi[:, :, None, :]
    m_im = cr[None] * qi[:, :, None, :] + ci[None] * qr[:, :, None, :]
    top = jnp.concatenate([jnp.transpose(m_re, (1, 3, 0, 2)), -jnp.transpose(m_im, (1, 3, 0, 2))], 1)
    top = top.reshape(G, 2 * P, n * SG)
    w_re = pw_re[:n, :, :, None] * bb_re[None] - pw_im[:n, :, :, None] * bb_im[None]
    w_im = pw_re[:n, :, :, None] * bb_im[None] + pw_im[:n, :, :, None] * bb_re[None]
    ktap = (jnp.einsum('gcp,tgpd->tgcd', cr, w_re, precision=HI)
            - jnp.einsum('gcp,tgpd->tgcd', ci, w_im, precision=HI))
    rows = []
    for s in range(n):
        cols = []
        for i in range(n):
            cols.append(jnp.transpose(ktap[i - s], (0, 2, 1)) if s <= i else jnp.zeros((G, SG, SG), F32))
        rows.append(jnp.concatenate(cols, -1))
    toep = jnp.concatenate(rows, 1)
    CT = jnp.concatenate([top, toep], 1)
    z64 = jnp.zeros((G // 2, n * SG, P), F32)
    e, o = B8[0::2], B8[1::2]
    B8p = jnp.concatenate([
        jnp.concatenate([e[..., :P], z64, e[..., P:], z64], -1),
        jnp.concatenate([z64, o[..., :P], z64, o[..., P:]], -1)], 1)
    ce, co = CT[0::2], CT[1::2]
    zc = lambda r: jnp.zeros((G // 2, r, n * SG), F32)
    CTp = jnp.concatenate([
        jnp.concatenate([ce[:, :P], zc(P)], -1), jnp.concatenate([zc(P), co[:, :P]], -1),
        jnp.concatenate([ce[:, P:2 * P], zc(P)], -1), jnp.concatenate([zc(P), co[:, P:2 * P]], -1),
        jnp.concatenate([ce[:, 2 * P:], zc(n * SG)], -1), jnp.concatenate([zc(n * SG), co[:, 2 * P:]], -1)], 1)
    l8r, l8i = pw_re[n].reshape(1, G * P), pw_im[n].reshape(1, G * P)
    return B8p, CTp, l8r, l8i, d_skip.astype(F32).reshape(1, G * SG)


def _s5_kernel(x_ref, d_ref, h0r_ref, h0i_ref, b8_ref, ct_ref, lr_ref, li_ref,
               z_ref, hpr_ref, hpi_ref, hsr_ref, hsi_ref, u_scr, v_scr, h_scr, y_scr, *, bp, nk, bs):
    W = LANES
    n = S5_STEPS
    SG = W // n
    rp = bp * nk
    ra = rp + bs
    rc_rows = _pick(ra, (64, 32, 16, 8))
    chunk = lax.broadcasted_iota(jnp.int32, (rc_rows, W), 1) // SG

    def chunk_transpose(vs):
        vs = list(vs)
        d = 1
        while d < n:
            hi_bit = (chunk // d) % 2 == 1
            for i in range(n):
                if (i // d) % 2 == 0:
                    a, b = vs[i], vs[i + d]
                    vs[i] = jnp.where(hi_bit, pltpu.roll(b, d * SG, 1), a)
                    vs[i + d] = jnp.where(hi_bit, b, pltpu.roll(a, W - d * SG, 1))
            d *= 2
        return vs

    def relayout_in(rc, c):
        r0 = pl.multiple_of(rc * rc_rows, rc_rows)
        us = chunk_transpose([x_ref[pl.ds(r0 * n + i, rc_rows, stride=n), :] for i in range(n)])
        for g in range(n):
            u_scr[g, pl.ds(r0, rc_rows), :] = us[g]
        return c

    lax.fori_loop(0, ra // rc_rows, relayout_in, 0)

    npair = n // 2
    for p in range(npair):
        ub = jnp.concatenate([u_scr[2 * p], u_scr[2 * p + 1]], 1).astype(BF16)
        v = jnp.dot(ub, b8_ref[p].astype(BF16), preferred_element_type=F32)
        v_scr[2 * p] = v[:rp, :W]
        v_scr[2 * p + 1] = v[:rp, W:]
        sl = slice(p * W, (p + 1) * W)
        h0r, h0i = h0r_ref[:, sl], h0i_ref[:, sl]
        lr, li = lr_ref[:, sl], li_ref[:, sl]
        h_scr[2 * p, rp:, :] = h0r
        h_scr[2 * p + 1, rp:, :] = h0i
        hsr_ref[:, sl] = lr * h0r - li * h0i + v[rp:, :W]
        hsi_ref[:, sl] = lr * h0i + li * h0r + v[rp:, W:]

    lrs = [lr_ref[:, p * W:(p + 1) * W] for p in range(npair)]
    lis = [li_ref[:, p * W:(p + 1) * W] for p in range(npair)]

    def step(k, carry):
        out = []
        for p in range(npair):
            hr, hi = carry[2 * p], carry[2 * p + 1]
            h_scr[2 * p, pl.ds(k, bp, stride=nk), :] = hr
            h_scr[2 * p + 1, pl.ds(k, bp, stride=nk), :] = hi
            vr = v_scr[2 * p, pl.ds(k, bp, stride=nk), :]
            vi = v_scr[2 * p + 1, pl.ds(k, bp, stride=nk), :]
            out.append(lrs[p] * hr - lis[p] * hi + vr)
            out.append(lrs[p] * hi + lis[p] * hr + vi)
        return tuple(out)

    fin = lax.fori_loop(0, nk, step, tuple(jnp.zeros((bp, W), F32) for _ in range(n)), unroll=4)
    for p in range(npair):
        hpr_ref[:, p * W:(p + 1) * W] = fin[2 * p]
        hpi_ref[:, p * W:(p + 1) * W] = fin[2 * p + 1]

    for p in range(npair):
        lhs = jnp.concatenate([h_scr[2 * p], h_scr[2 * p + 1], u_scr[2 * p], u_scr[2 * p + 1]], 1)
        y = jnp.dot(lhs.astype(BF16), ct_ref[p].astype(BF16), preferred_element_type=F32)
        y_scr[2 * p] = y[:, :W]
        y_scr[2 * p + 1] = y[:, W:]

    dv = d_ref[...]

    def relayout_out(rc, c):
        r0 = pl.multiple_of(rc * rc_rows, rc_rows)
        zs = chunk_transpose([y_scr[g, pl.ds(r0, rc_rows), :] for g in range(n)])
        for i in range(n):
            a = x_ref[pl.ds(r0 * n + i, rc_rows, stride=n), :]
            z_ref[pl.ds(r0 * n + i, rc_rows, stride=n), :] = jax.nn.gelu(zs[i] + dv * a)
        return c

    lax.fori_loop(0, ra // rc_rows, relayout_out, 0)


def _s5_mixer(h, shape_p, shape_s, state_s, prm):
    B8p, CTp, l8r, l8i, dvec = prm
    T, D = h.shape
    G = 2 * B8p.shape[0]
    P = LANES // 2
    bp, lp = shape_p
    bs, ls = shape_s
    nk = lp // S5_STEPS
    ra = T // S5_STEPS
    assert ls == S5_STEPS and lp % S5_STEPS == 0 and ra % SUBLANES == 0 and bp * nk + bs == ra
    h0r = state_s[..., 0].astype(F32).reshape(bs, G * P)
    h0i = state_s[..., 1].astype(F32).reshape(bs, G * P)
    gpb = S5_STEPS
    sw = gpb * P
    xcol = pl.BlockSpec((T, LANES), lambda j: (0, j))
    st_s = pl.BlockSpec((bs, sw), lambda j: (0, j))
    st_p = pl.BlockSpec((bp, sw), lambda j: (0, j))
    vec = pl.BlockSpec((1, sw), lambda j: (0, j))
    z, hpr, hpi, hsr, hsi = pl.pallas_call(
        functools.partial(_s5_kernel, bp=bp, nk=nk, bs=bs),
        grid=(G // gpb,),
        in_specs=[xcol, pl.BlockSpec((1, LANES), lambda j: (0, j)), st_s, st_s,
                  pl.BlockSpec((gpb // 2, 2 * LANES, 2 * LANES), lambda j: (j, 0, 0)),
                  pl.BlockSpec((gpb // 2, 4 * LANES, 2 * LANES), lambda j: (j, 0, 0)),
                  vec, vec],
        out_specs=[xcol, st_p, st_p, st_s, st_s],
        out_shape=[jax.ShapeDtypeStruct((T, D), F32),
                   jax.ShapeDtypeStruct((bp, G * P), F32), jax.ShapeDtypeStruct((bp, G * P), F32),
                   jax.ShapeDtypeStruct((bs, G * P), F32), jax.ShapeDtypeStruct((bs, G * P), F32)],
        scratch_shapes=[pltpu.VMEM((gpb, ra, LANES), F32), pltpu.VMEM((gpb, bp * nk, LANES), F32),
                        pltpu.VMEM((gpb, ra, LANES), F32), pltpu.VMEM((gpb, ra, LANES), F32)],
        compiler_params=_cparams(("parallel",)))(h, dvec, h0r, h0i, B8p, CTp, l8r, l8i)
    st = lambda r, i, nb: jnp.stack([r.reshape(nb, G, P), i.reshape(nb, G, P)], -1)
    return z, st(hpr, hpi, bp), st(hsr, hsi, bs)


def _conv_kernel(x_ref, halo_ref, buf_ref, cw_ref, o_ref, *, seq_rows, n_norm_q, n_norm_k, q_scale, first_axis):
    x = x_ref[...]
    R, CB = x.shape
    width = cw_ref.shape[0]
    row = lax.broadcasted_iota(jnp.int32, (R, CB), 0)
    if seq_rows == SUBLANES:
        prev = buf_ref[...]
        trow = row % SUBLANES
    else:
        prev = jnp.where(pl.program_id(first_axis) == 0, buf_ref[...], halo_ref[...])
        trow = row
    acc = x * cw_ref[width - 1:width, :]
    for j in range(1, width):
        s = pltpu.roll(x, j, 0)
        if seq_rows == SUBLANES:
            p = pltpu.roll(prev, (R - SUBLANES + j) % R, 0)
            s = jnp.where(trow < j, p, s)
        else:
            p = pltpu.roll(prev, j, 0)
            row8 = lax.broadcasted_iota(jnp.int32, (SUBLANES, CB), 0)
            top = jnp.where(row8 < j, p, s[:SUBLANES])
            s = jnp.concatenate([top, s[SUBLANES:]], 0) if R > SUBLANES else top
        acc = acc + s * cw_ref[width - 1 - j:width - j, :]
    c = acc * jax.nn.sigmoid(acc)
    cb = pl.program_id(2 if seq_rows != SUBLANES else 1)
    segs = []
    for hh in range(CB // LANES):
        seg = c[:, hh * LANES:(hh + 1) * LANES]
        segs.append(seg * lax.rsqrt(jnp.sum(seg * seg, axis=-1, keepdims=True) + 1e-6))
    normed = jnp.concatenate(segs, -1) if len(segs) > 1 else segs[0]
    scale = jnp.where(cb < n_norm_q, jnp.float32(q_scale), jnp.float32(1.0))
    o_ref[...] = jnp.where(cb < n_norm_q + n_norm_k, normed * scale, c)


def _conv_qkv(proj, row0, conv_buf, conv_w, bt, L, conv_dim, key_dim, head_dim):
    width = conv_w.shape[0]
    assert width - 1 <= SUBLANES and L % SUBLANES == 0
    CB = _pick(key_dim, (512, 256, 128))
    ncb = conv_dim // CB
    nq = nk = key_dim // CB
    buf8 = jnp.concatenate([jnp.zeros((bt, SUBLANES - (width - 1), conv_dim), F32), conv_buf.astype(F32)], 1)
    buf8 = buf8.reshape(bt * SUBLANES, conv_dim)
    cw = conv_w.astype(F32)
    q_scale = float(head_dim) ** -0.5
    if L == SUBLANES:
        R = _pick(bt * L, (128, 64, 32, 16, 8))
        assert row0 % R == 0
        blk = pl.BlockSpec((R, CB), lambda i, c: (i, c))
        xin = pl.BlockSpec((R, CB), lambda i, c: (i + row0 // R, c))
        return pl.pallas_call(
            functools.partial(_conv_kernel, seq_rows=SUBLANES, n_norm_q=nq, n_norm_k=nk, q_scale=q_scale,
                              first_axis=0),
            grid=(bt * L // R, ncb),
            in_specs=[xin, blk, blk, pl.BlockSpec((width, CB), lambda i, c: (0, c))],
            out_specs=blk, out_shape=jax.ShapeDtypeStruct((bt * L, conv_dim), F32),
            compiler_params=_cparams(("parallel", "parallel")))(proj, buf8, buf8, cw)
    R = _pick(L, (256, 128, 64, 32, 16, 8))
    nt = L // R
    assert row0 % R == 0
    rb0 = row0 // R
    blk = pl.BlockSpec((R, CB), lambda b, t, c: (b * nt + t, c))
    xin = pl.BlockSpec((R, CB), lambda b, t, c: (rb0 + b * nt + t, c))
    halo = pl.BlockSpec((SUBLANES, CB),
                        lambda b, t, c: (jnp.maximum((rb0 + b * nt + t) * (R // SUBLANES) - 1, 0), c))
    bufs = pl.BlockSpec((SUBLANES, CB), lambda b, t, c: (b, c))
    return pl.pallas_call(
        functools.partial(_conv_kernel, seq_rows=R, n_norm_q=nq, n_norm_k=nk, q_scale=q_scale, first_axis=1),
        grid=(bt, nt, ncb),
        in_specs=[xin, halo, bufs, pl.BlockSpec((width, CB), lambda b, t, c: (0, c))],
        out_specs=blk, out_shape=jax.ShapeDtypeStruct((bt * L, conv_dim), F32),
        compiler_params=_cparams(("parallel", "parallel", "parallel")))(proj, proj, buf8, cw)


def _dot_nt(a, b):
    return lax.dot_general(a.astype(BF16), b.astype(BF16), (((1,), (1,)), ((), ())), preferred_element_type=F32)


def _dot_tn(a, b):
    return lax.dot_general(a.astype(BF16), b.astype(BF16), (((0,), (0,)), ((), ())), preferred_element_type=F32)


def _dot(a, b):
    return jnp.dot(a.astype(BF16), b.astype(BF16), preferred_element_type=F32)


def _delta_kernel(q_ref, k_ref, v_ref, z_ref, col_ref, rowp_ref, s0_ref, ng_ref, o_ref, so_ref, s_scr,
                  *, hpg, C):
    n = pl.program_id(2)

    @pl.when(n == 0)
    def _():
        s_scr[...] = s0_ref[0]

    ri = lax.broadcasted_iota(jnp.int32, (C, C), 0)
    ci = lax.broadcasted_iota(jnp.int32, (C, C), 1)
    incl = ri >= ci
    strict = ri > ci
    eye = (ri == ci).astype(F32)
    col = col_ref[0, 0]
    rowp = rowp_ref[0, 0, 0]
    ng = ng_ref[...]
    rep = hpg // (q_ref.shape[-1] // LANES)
    hs = range(hpg)
    lanes = lambda i: slice(i * LANES, (i + 1) * LANES)
    qs = [q_ref[:, lanes(i)] for i in range(hpg // rep)]
    ks = [k_ref[:, lanes(i)] for i in range(hpg // rep)]
    gc = [col[:, h:h + 1] for h in hs]
    beta = [col[:, hpg + h:hpg + h + 1] for h in hs]
    eg = [jnp.exp(gc[h]) for h in hs]
    decay = [jnp.where(incl, jnp.exp(jnp.where(incl, gc[h] - rowp[h:h + 1, :], 0.0)), 0.0) for h in hs]
    kb = [ks[h // rep] * beta[h] for h in hs]
    m = [jnp.where(strict, _dot_nt(kb[h], ks[h // rep]) * decay[h], 0.0) for h in hs]
    qk = [_dot_nt(qs[i], ks[i]) for i in range(hpg // rep)]
    blk2 = (ri // 2) == (ci // 2)
    tinv = [eye - jnp.where(blk2, m[h], 0.0) for h in hs]
    s = 2
    while s < C:
        off = ((ri // (2 * s)) == (ci // (2 * s))) & ((ri // s) % 2 == 1) & ((ci // s) % 2 == 0)
        tb = [_dot(tinv[h], jnp.where(off, m[h], 0.0)) for h in hs]
        tinv = [tinv[h] - _dot(tb[h], tinv[h]) for h in hs]
        s *= 2
    u = [_dot(tinv[h], v_ref[:, lanes(h)] * beta[h]) for h in hs]
    w = [_dot(tinv[h], kb[h] * eg[h]) for h in hs]
    S = [s_scr[h] for h in hs]
    ws = [_dot(jnp.concatenate([w[h], qs[h // rep] * eg[h]], 0), S[h]) for h in hs]
    v_new = [u[h] - ws[h][:C] for h in hs]
    o = [ws[h][C:] + _dot(jnp.where(incl, qk[h // rep] * decay[h], 0.0), v_new[h]) for h in hs]
    g_last = [gc[h][C - 1:C, :] for h in hs]
    s_new = [S[h] * jnp.exp(g_last[h]) + _dot_tn(ks[h // rep] * jnp.exp(g_last[h] - gc[h]), v_new[h]) for h in hs]
    for h in hs:
        s_scr[h] = s_new[h]
        zh = z_ref[:, lanes(h)]
        on = o[h] * lax.rsqrt(jnp.mean(o[h] * o[h], axis=-1, keepdims=True) + NORM_EPS) * ng
        o_ref[:, lanes(h)] = (on * (zh * jax.nn.sigmoid(zh))).astype(o_ref.dtype)

    @pl.when(n == pl.num_programs(2) - 1)
    def _():
        so_ref[0] = s_scr[...]


def _delta_rule(qkvc, proj, row0, ba, s0, s_layer, a_log, dt_bias, norm_g, bt, L, n_heads, key_dim, val_dim,
                chunk, hpg):
    H = n_heads
    C = chunk if L % chunk == 0 else L
    N = L // C
    hpg = min(hpg, H)
    HG = H // hpg
    assert row0 % C == 0 and hpg % SUBLANES == 0
    conv_dim = 2 * key_dim + val_dim
    rep = H // (key_dim // LANES)
    qw = (hpg // rep) * LANES
    vw = hpg * LANES
    beta = jax.nn.sigmoid(ba[:, :H])
    g = -jnp.exp(a_log.astype(F32)) * jax.nn.softplus(ba[:, H:2 * H] + dt_bias.astype(F32))
    gcum = jnp.cumsum(g.reshape(bt, N, C, H), axis=2)
    col = jnp.concatenate([gcum.reshape(bt, L, HG, hpg), beta.reshape(bt, L, HG, hpg)], -1)
    col = col.transpose(0, 2, 1, 3)
    rowp = gcum.reshape(bt, N, C, HG, hpg).transpose(0, 3, 1, 4, 2)
    kq0 = key_dim // qw
    v0 = 2 * key_dim // vw
    z0 = conv_dim // vw
    rb0 = row0 // C
    o, s_out = pl.pallas_call(
        functools.partial(_delta_kernel, hpg=hpg, C=C),
        grid=(bt, HG, N),
        in_specs=[pl.BlockSpec((C, qw), lambda b, h, n: (b * N + n, h)),
                  pl.BlockSpec((C, qw), lambda b, h, n: (b * N + n, kq0 + h)),
                  pl.BlockSpec((C, vw), lambda b, h, n: (b * N + n, v0 + h)),
                  pl.BlockSpec((C, vw), lambda b, h, n: (rb0 + b * N + n, z0 + h)),
                  pl.BlockSpec((1, 1, C, 2 * hpg), lambda b, h, n: (b, h, n, 0)),
                  pl.BlockSpec((1, 1, 1, hpg, C), lambda b, h, n: (b, h, n, 0, 0)),
                  pl.BlockSpec((None, 1, hpg, LANES, LANES), lambda b, h, n: (s_layer, b, h, 0, 0)),
                  pl.BlockSpec((1, LANES), lambda b, h, n: (0, 0))],
        out_specs=[pl.BlockSpec((C, vw), lambda b, h, n: (b * N + n, h)),
                   pl.BlockSpec((1, hpg, LANES, LANES), lambda b, h, n: (b, h, 0, 0))],
        out_shape=[jax.ShapeDtypeStruct((bt * L, val_dim), BF16 if C % 16 == 0 else F32),
                   jax.ShapeDtypeStruct((bt, H, LANES, LANES), F32)],
        scratch_shapes=[pltpu.VMEM((hpg, LANES, LANES), F32)],
        compiler_params=_cparams(("parallel", "parallel", "arbitrary")))(
            qkvc, qkvc, qkvc, proj, col, rowp, s0.astype(F32), norm_g.astype(F32).reshape(1, LANES))
    return o.astype(BF16), s_out


def _expert_kernel(be_ref, bn_ref, tok_ref, x_hbm, wg_ref, wu_ref, wd_ref, o_ref, xg, acc_g, acc_u, hbuf, sem,
                   *, nkc):
    i = pl.program_id(0)
    c = pl.program_id(1)
    rows = bn_ref[i]
    valid = rows > 0
    bm = acc_g.shape[0]
    nt = xg.shape[0] // bm
    kc = wg_ref.shape[1]

    def token_copy(tok, r):
        return pltpu.make_async_copy(x_hbm.at[pl.ds(pl.multiple_of(tok * nt, nt), nt), :],
                                     xg.at[pl.ds(pl.multiple_of(r * nt, nt), nt), :], sem)

    def start_gather(blk):
        def start(r, carry):
            token_copy(tok_ref[blk * bm + r], r).start()
            return carry

        lax.fori_loop(0, bn_ref[blk], start, 0)

    @pl.when((i == 0) & (c == 0))
    def _():
        xg[...] = jnp.zeros_like(xg)
        start_gather(0)

    @pl.when((c == nkc) & (i + 1 < pl.num_programs(0)))
    def _():
        start_gather(i + 1)

    @pl.when(valid & (c == 0))
    def _():
        def wait(r, carry):
            token_copy(0, r).wait()
            return carry

        lax.fori_loop(0, rows, wait, 0)

    def unpack(tile):
        w = xg[pl.ds(tile % nt, bm, stride=nt), :]
        bits = (w << 16) if tile < nt else (w & jnp.uint32(0xFFFF0000))
        return lax.bitcast_convert_type(bits, F32).astype(BF16)

    for cc in range(nkc):
        @pl.when(valid & (c == cc))
        def _(cc=cc):
            tiles = range(cc * (kc // LANES), (cc + 1) * (kc // LANES))
            xk = jnp.concatenate([unpack(t) for t in tiles], axis=1)
            pg = jnp.dot(xk, wg_ref[0].astype(BF16), preferred_element_type=F32)
            pu = jnp.dot(xk, wu_ref[0].astype(BF16), preferred_element_type=F32)
            if cc == 0:
                acc_g[...] = pg
                acc_u[...] = pu
            else:
                acc_g[...] += pg
                acc_u[...] += pu
            if cc == nkc - 1:
                gte = acc_g[...]
                hbuf[...] = (gte * jax.nn.sigmoid(gte) * acc_u[...]).astype(BF16)

    @pl.when(valid & (c >= nkc))
    def _():
        o_ref[...] = jnp.dot(hbuf[...], wd_ref[0].astype(BF16), preferred_element_type=F32)

    @pl.when(jnp.logical_not(valid) & (c >= nkc))
    def _():
        o_ref[...] = jnp.zeros_like(o_ref)


def _experts(xp, row_tok, blk_exp, blk_rows, w_gate, w_up, w_down, layer):
    R = row_tok.shape[0]
    _, E, D, FF = w_gate.shape
    nt = D // 2 // LANES
    NB = R // MOE_ROWS
    kc = _pick(D, (1024, 512, 256, 128))
    nkc = D // kc
    nc = min(2 * kc, D)
    nnc = D // nc

    def kidx(c, v):
        return jnp.where(v > 0, jnp.minimum(c, nkc - 1), nkc - 1)

    def nidx(c, v):
        return jnp.where(v > 0, jnp.maximum(c - nkc, 0), nnc - 1)

    grid_spec = pltpu.PrefetchScalarGridSpec(
        num_scalar_prefetch=3, grid=(NB, nkc + nnc),
        in_specs=[pl.BlockSpec(memory_space=pl.ANY),
                  pl.BlockSpec((None, 1, kc, FF), lambda i, c, be, bn, tk: (layer, be[i], kidx(c, bn[i]), 0)),
                  pl.BlockSpec((None, 1, kc, FF), lambda i, c, be, bn, tk: (layer, be[i], kidx(c, bn[i]), 0)),
                  pl.BlockSpec((None, 1, FF, nc), lambda i, c, be, bn, tk: (layer, be[i], 0, nidx(c, bn[i])))],
        out_specs=pl.BlockSpec((MOE_ROWS, nc), lambda i, c, be, bn, tk: (i, jnp.maximum(c - nkc, 0))),
        scratch_shapes=[pltpu.VMEM((MOE_ROWS * nt, LANES), jnp.uint32),
                        pltpu.VMEM((MOE_ROWS, FF), F32), pltpu.VMEM((MOE_ROWS, FF), F32),
                        pltpu.VMEM((MOE_ROWS, FF), BF16), pltpu.SemaphoreType.DMA])
    return pl.pallas_call(
        functools.partial(_expert_kernel, nkc=nkc), grid_spec=grid_spec,
        out_shape=jax.ShapeDtypeStruct((R, D), F32),
        compiler_params=_cparams(("arbitrary", "arbitrary")))(blk_exp, blk_rows, row_tok, xp, w_gate, w_up, w_down)


def _moe(h2p, eidx, w_gate, w_up, w_down, layer):
    T = eidx.shape[0]
    E = w_gate.shape[1]
    A = T * MOE_TOPK
    BM = MOE_ROWS
    flat_e = eidx.reshape(A)
    order = jnp.argsort(flat_e)
    e_sorted = flat_e[order]
    counts = jnp.zeros((E,), jnp.int32).at[flat_e].add(1)
    padded = (counts + BM - 1) // BM * BM
    pad_end = jnp.cumsum(padded)
    pad_start = pad_end - padded
    start = jnp.cumsum(counts) - counts
    dest = pad_start[e_sorted] + jnp.arange(A, dtype=jnp.int32) - start[e_sorted]
    NB = (A + E * (BM - 1)) // BM
    R = NB * BM
    row_tok = jnp.zeros((R,), jnp.int32).at[dest].set((order // MOE_TOPK).astype(jnp.int32))
    pos = jnp.zeros((A,), jnp.int32).at[order].set(dest)
    blk_start = jnp.arange(NB, dtype=jnp.int32) * BM
    last_used = jnp.maximum(pad_end[-1] // BM - 1, 0)
    blk_exp = jnp.minimum(jnp.searchsorted(pad_end, jnp.minimum(blk_start, last_used * BM), side='right'),
                          E - 1).astype(jnp.int32)
    blk_rows = jnp.where(blk_start < pad_end[-1],
                         jnp.clip(pad_start[blk_exp] + counts[blk_exp] - blk_start, 0, BM), 0).astype(jnp.int32)
    ys = _experts(h2p, row_tok, blk_exp, blk_rows, w_gate, w_up, w_down, layer)
    return ys, pos


def kernel(x_prompt, x_sample, state_ssm, state_conv, state_gdn, attn_norm, ffn_norm, final_norm,
           ssm_lambda_re, ssm_lambda_im, ssm_log_dt, ssm_b_re, ssm_b_im, ssm_c_re, ssm_c_im, ssm_d,
           ssm_w_glu, ssm_b_glu, gdn_w_in, gdn_conv_w, gdn_a_log, gdn_dt_bias, gdn_norm, gdn_w_out,
           moe_w_grp, moe_b_grp, moe_w_exp, moe_b_exp, moe_w_gate, moe_w_up, moe_w_down):
    bp, lp, D = x_prompt.shape
    bs, ls, _ = x_sample.shape
    n_p, n_s = bp * lp, bs * ls
    depth = attn_norm.shape[0]
    n_mix = 2
    H = gdn_a_log.shape[1]
    head_dim = gdn_norm.shape[1]
    val_dim = H * head_dim
    conv_dim = gdn_conv_w.shape[2]
    key_dim = (conv_dim - val_dim) // 2
    n_grp = moe_w_grp.shape[2]
    n_exp = moe_w_exp.shape[2]
    n_per = n_exp // n_grp
    assert head_dim == LANES and n_grp + n_exp <= LANES

    x = jnp.concatenate([x_prompt.reshape(n_p, D), x_sample.reshape(n_s, D)], 0).astype(F32)
    delta = None
    ssm_p, ssm_s, conv_p, conv_s, gdn_p, gdn_s = [], [], [], [], [], []
    for i in range(depth):
        j = i // n_mix
        if i % n_mix == 0:
            x, h = _add_norm(x, delta, attn_norm[i], F32)
            prm = _s5_params(ssm_lambda_re[j], ssm_lambda_im[j], ssm_log_dt[j], ssm_b_re[j], ssm_b_im[j],
                             ssm_c_re[j], ssm_c_im[j], ssm_d[j])
            z, hp_new, hs_new = _s5_mixer(h, (bp, lp), (bs, ls), state_ssm[j], prm)
            ssm_p.append(hp_new.astype(x.dtype))
            ssm_s.append(hs_new.astype(x.dtype))
            x = _matmul(z.astype(BF16), ssm_w_glu, D, mode="glu", layer=j, res=x, z=z,
                        bias=ssm_b_glu[j].astype(F32))
        else:
            x, h = _add_norm(x, delta, attn_norm[i], BF16)
            n_main = conv_dim + val_dim
            proj = _matmul(h, gdn_w_in, n_main, layer=j)
            w_ba = jnp.concatenate([gdn_w_in[j, :, n_main:], jnp.zeros((D, LANES - 2 * H), F32)], 1)
            ba = _matmul(h, w_ba, LANES, tn=LANES)
            outs = []
            width = gdn_conv_w.shape[1]
            for (r0, bt, L, buf, s0, s_layer, cl, sl, hpg) in (
                    (0, bp, lp, jnp.zeros((bp,) + state_conv.shape[2:], F32),
                     jnp.zeros((1, bp) + state_gdn.shape[2:], F32), 0, conv_p, gdn_p, 16),
                    (n_p, bs, ls, state_conv[j], state_gdn, j, conv_s, gdn_s, 32)):
                qkvc = _conv_qkv(proj, r0, buf, gdn_conv_w[j], bt, L, conv_dim, key_dim, head_dim)
                o, s_new = _delta_rule(qkvc, proj, r0, ba[r0:r0 + bt * L], s0, s_layer, gdn_a_log[j],
                                       gdn_dt_bias[j], gdn_norm[j], bt, L, H, key_dim, val_dim, 64, hpg)
                outs.append(o)
                last = (r0 + (jnp.arange(bt) * L)[:, None] + jnp.arange(L - (width - 1), L)[None, :]).reshape(-1)
                cl.append(jnp.take(proj, last, axis=0)[:, :conv_dim].reshape(bt, width - 1, conv_dim))
                sl.append(s_new)
            x = _matmul(jnp.concatenate(outs, 0), gdn_w_out, D, mode="res", layer=j, res=x)
        wr = jnp.concatenate([moe_w_grp[i], moe_w_exp[i],
                              jnp.zeros((D, LANES - n_grp - n_exp), F32)], 1)
        br = jnp.concatenate([moe_b_grp[i], moe_b_exp[i], jnp.zeros((LANES - n_grp - n_exp,), F32)])
        h2p, eidx, wts = _add_norm(x, None, ffn_norm[i], jnp.uint32,
                                   router=(wr, br.reshape(1, LANES), n_grp, n_per))[1:]
        ys, pos = _moe(h2p, eidx[:, :MOE_TOPK], moe_w_gate, moe_w_up, moe_w_down, i)
        delta = (ys, pos, wts)
    x, y = _add_norm(x, delta, final_norm, F32)
    return (y[:n_p].reshape(bp, lp, D), y[n_p:].reshape(bs, ls, D),
            jnp.stack(ssm_p), jnp.stack(conv_p), jnp.stack(gdn_p),
            jnp.stack(ssm_s), jnp.stack(conv_s), jnp.stack(gdn_s))
```
